```python
import jax, jax.numpy as jnp
from jax import lax
import numpy as np

D_MODEL = 1024
BATCH = 32
SEQ = 2048
DEPTH = 1
DEC_BATCH = 1
DEC_SEQ = 16384
PAST_LEN = 128

RET_HEADS = 4
RET_DK = 128
RET_DV = 128
RET_CHUNK = 128
ROPE_BASE = 10000.0
DN_HEADS = 4
DN_DK = 128
DN_DV = 128
DN_CHUNK = 64
DN_CONV = 5
PEER_HEADS = 8
PEER_DK = 256
PEER_NKEYS = 128
PEER_TOPK = 16
PEER_EXPERTS = PEER_NKEYS * PEER_NKEYS
PEER_BLOCK = 128
NORM_EPS = 1e-6
DN_QKV = 2 * DN_HEADS * DN_DK + DN_HEADS * DN_DV
IN_SPLITS = (RET_HEADS * RET_DK, RET_HEADS * RET_DK, RET_HEADS * RET_DV, RET_HEADS * RET_DV,
             DN_QKV, DN_HEADS * DN_DV, DN_HEADS, DN_HEADS, DN_HEADS, DN_HEADS, D_MODEL, D_MODEL)
IN_WIDTH = sum(IN_SPLITS)

kernel_name = "hybrid_retention_gdn_peer_encoder"


def rms_norm(x, g):
    xf = x.astype(jnp.float32)
    y = xf * lax.rsqrt(jnp.mean(xf * xf, axis=-1, keepdims=True) + NORM_EPS)
    return (y * g.astype(jnp.float32)).astype(x.dtype)


def to_chunks(x, c):
    b, s, h, d = x.shape
    return x.reshape(b, s // c, c, h, d).transpose(1, 0, 3, 2, 4)


def from_chunks(x):
    n, b, h, c, d = x.shape
    return x.transpose(1, 0, 3, 2, 4).reshape(b, n * c, h, d)


def rotary(x):
    s, d = x.shape[1], x.shape[-1]
    half = d // 2
    inv_freq = 1.0 / (ROPE_BASE ** jnp.linspace(0.0, 1.0, half, dtype=jnp.float32))
    ang = jnp.arange(s, dtype=jnp.float32)[:, None] * inv_freq[None, :]
    cos = jnp.cos(ang)[None, :, None, :]
    sin = jnp.sin(ang)[None, :, None, :]
    x1, x2 = x[..., :half], x[..., half:]
    return jnp.concatenate([x1 * cos - x2 * sin, x2 * cos + x1 * sin], axis=-1)


def retention_dir(q, k, v, log_gamma, strict):
    qc, kc, vc = to_chunks(q, RET_CHUNK), to_chunks(k, RET_CHUNK), to_chunks(v, RET_CHUNK)
    n, b, h, c, dk = qc.shape
    dv = vc.shape[-1]
    pos = jnp.arange(c, dtype=jnp.float32)
    diff = pos[:, None] - pos[None, :]
    keep = (diff > 0) if strict else (diff >= 0)
    decay = jnp.where(keep, jnp.exp(log_gamma[:, None, None] * jnp.maximum(diff, 0.0)), 0.0)
    scores = jnp.einsum('nbhid,nbhjd->nbhij', qc, kc) * decay
    o_inner = jnp.einsum('nbhij,nbhjv->nbhiv', scores, vc)
    zeta = jnp.exp(log_gamma[:, None] * (c - 1 - pos))
    xi = jnp.exp(log_gamma[:, None] * (pos + 1.0))
    chunk_state = jnp.einsum('nbhcd,nbhcv->nbhdv', kc * zeta[:, :, None], vc)
    carry_decay = jnp.exp(log_gamma * c)[:, None, None]

    def step(state, contrib):
        return state * carry_decay + contrib, state

    _, prev = lax.scan(step, jnp.zeros((b, h, dk, dv), jnp.float32), chunk_state)
    o_cross = jnp.einsum('nbhcd,nbhdv->nbhcv', qc, prev) * xi[:, :, None]
    return from_chunks(o_inner + o_cross)


def gated_delta_dir(q, k, v, g, beta):
    qc, kc, vc = to_chunks(q, DN_CHUNK), to_chunks(k, DN_CHUNK), to_chunks(v, DN_CHUNK)
    gc = to_chunks(g[..., None], DN_CHUNK)[..., 0]
    bc = to_chunks(beta[..., None], DN_CHUNK)[..., 0]
    n, b, h, c, dk = qc.shape
    dv = vc.shape[-1]
    gcum = jnp.cumsum(gc, axis=-1)
    tril = jnp.tril(jnp.ones((c, c), bool))
    strict = jnp.tril(jnp.ones((c, c), bool), -1)
    diff = gcum[..., :, None] - gcum[..., None, :]
    L = jnp.where(tril, jnp.exp(jnp.where(tril, diff, 0.0)), 0.0)
    kb = kc * bc[..., None]
    a = jnp.where(strict, jnp.einsum('nbhid,nbhjd->nbhij', kb, kc) * L, 0.0)
    rhs = jnp.concatenate([vc * bc[..., None], kb * jnp.exp(gcum)[..., None]], axis=-1)
    sol = lax.linalg.triangular_solve(a, rhs, left_side=True, lower=True, unit_diagonal=True)
    u, w = sol[..., :dv], sol[..., dv:]
    attn = jnp.einsum('nbhid,nbhjd->nbhij', qc, kc) * L
    q_dec = qc * jnp.exp(gcum)[..., None]
    k_dec = kc * jnp.exp(gcum[..., -1:] - gcum)[..., None]
    g_tot = jnp.exp(gcum[..., -1])[..., None, None]

    def step(S, inp):
        u_i, w_i, attn_i, qd_i, kd_i, gt_i = inp
        v_new = u_i - jnp.einsum('bhcd,bhdv->bhcv', w_i, S)
        o_i = jnp.einsum('bhcd,bhdv->bhcv', qd_i, S) + jnp.einsum('bhij,bhjv->bhiv', attn_i, v_new)
        S = S * gt_i + jnp.einsum('bhcd,bhcv->bhdv', kd_i, v_new)
        return S, o_i

    _, o = lax.scan(step, jnp.zeros((b, h, dk, dv), jnp.float32), (u, w, attn, q_dec, k_dec, g_tot))
    return from_chunks(o)


def flip(t):
    return jnp.flip(t, axis=1)


def retention_branch(q, k, v, gate, gn):
    b, s, _ = q.shape
    q = rotary(q.reshape(b, s, RET_HEADS, RET_DK)) * (RET_DK ** -0.5)
    k = rotary(k.reshape(b, s, RET_HEADS, RET_DK))
    v = v.reshape(b, s, RET_HEADS, RET_DV)
    lg = jnp.log(1.0 - 2.0 ** (-5.0 - jnp.arange(RET_HEADS, dtype=jnp.float32)))
    fwd = retention_dir(q, k, v, lg, False)
    bwd = flip(retention_dir(flip(q), flip(k), flip(v), lg[::-1], True))
    o = fwd + bwd
    mu = jnp.mean(o, axis=-1, keepdims=True)
    var = jnp.mean(jnp.square(o - mu), axis=-1, keepdims=True)
    o = (o - mu) * lax.rsqrt(var + NORM_EPS) * gn.astype(jnp.float32).reshape(RET_HEADS, RET_DV)
    return o.reshape(b, s, RET_HEADS * RET_DV) * jax.nn.silu(gate)


def short_conv(x, w):
    y = lax.conv_general_dilated(x, w.astype(x.dtype)[:, None, :], window_strides=(1,),
                                 padding=[(DN_CONV // 2, DN_CONV // 2)],
                                 dimension_numbers=('NWC', 'WIO', 'NWC'),
                                 feature_group_count=x.shape[-1])
    return jax.nn.silu(y)


def l2norm(t):
    return t * lax.rsqrt(jnp.sum(t * t, axis=-1, keepdims=True) + NORM_EPS)


def deltanet_branch(qkv, z, beta_f, beta_b, a_f, a_b, conv_w, a_log, dt_bias, norm_g):
    b, s, _ = qkv.shape
    qkv = short_conv(qkv, conv_w.astype(jnp.float32))
    nqk = DN_HEADS * DN_DK
    q = l2norm(qkv[..., :nqk].reshape(b, s, DN_HEADS, DN_DK)) * (DN_DK ** -0.5)
    k = l2norm(qkv[..., nqk:2 * nqk].reshape(b, s, DN_HEADS, DN_DK))
    v = qkv[..., 2 * nqk:].reshape(b, s, DN_HEADS, DN_DV)
    a_log = a_log.astype(jnp.float32)
    dt_bias = dt_bias.astype(jnp.float32)
    g_f = -jnp.exp(a_log[0]) * jax.nn.softplus(a_f + dt_bias[0])
    g_b = -jnp.exp(a_log[1]) * jax.nn.softplus(a_b + dt_bias[1])
    fwd = gated_delta_dir(q, k, v, g_f, jax.nn.sigmoid(beta_f))
    bwd = flip(gated_delta_dir(flip(q), flip(k), flip(v), flip(g_b), flip(jax.nn.sigmoid(beta_b))))
    o = fwd + bwd
    o = o * lax.rsqrt(jnp.mean(o * o, axis=-1, keepdims=True) + NORM_EPS) * norm_g.astype(jnp.float32)
    o = o * jax.nn.silu(z.reshape(b, s, DN_HEADS, DN_DV))
    return o.reshape(b, s, DN_HEADS * DN_DV)


def peer_ffn(h, wq, k1, k2, u, v):
    b, s, d = h.shape
    half = PEER_DK // 2

    def block(xb):
        t = xb.shape[0]
        q = (xb @ wq).reshape(t, PEER_HEADS, PEER_DK).astype(jnp.float32)
        s1 = jnp.einsum('thd,hnd->thn', q[..., :half], k1.astype(jnp.float32))
        s2 = jnp.einsum('thd,hnd->thn', q[..., half:], k2.astype(jnp.float32))
        v1, i1 = lax.top_k(s1, PEER_TOPK)
        v2, i2 = lax.top_k(s2, PEER_TOPK)
        cand = (v1[..., :, None] + v2[..., None, :]).reshape(t, PEER_HEADS, PEER_TOPK * PEER_TOPK)
        cand_idx = (i1[..., :, None] * PEER_NKEYS + i2[..., None, :]).reshape(t, PEER_HEADS, PEER_TOPK * PEER_TOPK)
        sc, sel = lax.top_k(cand, PEER_TOPK)
        eidx = jnp.take_along_axis(cand_idx, sel, axis=-1)
        gate = jax.nn.softmax(sc, axis=-1)
        act = jax.nn.gelu(jnp.einsum('td,thkd->thk', xb, u[eidx]).astype(jnp.float32), approximate=False)
        return jnp.einsum('thk,thkd->td', (gate * act).astype(xb.dtype), v[eidx])

    out = lax.map(block, h.reshape(-1, PEER_BLOCK, d))
    return out.reshape(b, s, d)


def encoder(x, g_mix, w_in, ret_gn, dn_conv, dn_a_log, dn_dt_bias, dn_norm, w_branch_a, w_branch_b,
            w_out, g_ffn, peer_wq, peer_k1, peer_k2, peer_u, peer_v, g_final):
    split_at = []
    acc = 0
    for sz in IN_SPLITS[:-1]:
        acc += sz
        split_at.append(acc)
    for l in range(DEPTH):
        h = rms_norm(x, g_mix[l])
        p = (h @ w_in[l]).astype(jnp.float32)
        qa, ka, va, ga, qkv_b, zb, bf, bb, af, ab, ma, mb = jnp.split(p, split_at, axis=-1)
        oa = retention_branch(qa, ka, va, ga, ret_gn[l])
        ob = deltanet_branch(qkv_b, zb, bf, bb, af, ab, dn_conv[l], dn_a_log[l], dn_dt_bias[l], dn_norm[l])
        ya = oa.astype(x.dtype) @ w_branch_a[l]
        yb = ob.astype(x.dtype) @ w_branch_b[l]
        merged = (jax.nn.sigmoid(ma) * ya + jax.nn.sigmoid(mb) * yb).astype(x.dtype)
        x = x + merged @ w_out[l]
        h2 = rms_norm(x, g_ffn[l])
        x = x + peer_ffn(h2, peer_wq[l], peer_k1[l], peer_k2[l], peer_u[l], peer_v[l]).astype(x.dtype)
    return rms_norm(x, g_final)


def setup_inputs(seed: int = 0) -> dict:
    key = jax.random.key(seed)
    ks = jax.random.split(key, 20)
    f32 = jnp.float32
    nrm = lambda k, shape, scale: jax.random.normal(k, shape, f32) * scale
    gain = lambda k, shape: 1.0 + 0.02 * jax.random.normal(k, shape, f32)
    dt = jnp.exp(jax.random.uniform(ks[7], (DEPTH, 2, DN_HEADS), f32, np.log(1e-3), np.log(1e-1)))
    return {
        "x_prompt": nrm(ks[0], (BATCH, SEQ, D_MODEL), 1.0),
        "x_sample": nrm(ks[1], (DEC_BATCH, DEC_SEQ, D_MODEL), 1.0),
        "g_mix": gain(ks[2], (DEPTH, D_MODEL)),
        "w_in": nrm(ks[3], (DEPTH, D_MODEL, IN_WIDTH), D_MODEL ** -0.5),
        "ret_gn": gain(ks[4], (DEPTH, RET_HEADS * RET_DV)),
        "dn_conv": nrm(ks[5], (DEPTH, DN_CONV, DN_QKV), DN_CONV ** -0.5),
        "dn_a_log": jnp.log(jax.random.uniform(ks[6], (DEPTH, 2, DN_HEADS), f32, 1.0, 16.0)),
        "dn_dt_bias": dt + jnp.log(-jnp.expm1(-dt)),
        "dn_norm": gain(ks[8], (DEPTH, DN_DV)),
        "w_branch_a": nrm(ks[9], (DEPTH, RET_HEADS * RET_DV, D_MODEL), (RET_HEADS * RET_DV) ** -0.5),
        "w_branch_b": nrm(ks[10], (DEPTH, DN_HEADS * DN_DV, D_MODEL), (DN_HEADS * DN_DV) ** -0.5),
        "w_out": nrm(ks[11], (DEPTH, D_MODEL, D_MODEL), D_MODEL ** -0.5),
        "g_ffn": gain(ks[12], (DEPTH, D_MODEL)),
        "peer_wq": nrm(ks[13], (DEPTH, D_MODEL, PEER_HEADS * PEER_DK), D_MODEL ** -0.5),
        "peer_k1": nrm(ks[14], (DEPTH, PEER_HEADS, PEER_NKEYS, PEER_DK // 2), (PEER_DK // 2) ** -0.5),
        "peer_k2": nrm(ks[15], (DEPTH, PEER_HEADS, PEER_NKEYS, PEER_DK // 2), (PEER_DK // 2) ** -0.5),
        "peer_u": nrm(ks[16], (DEPTH, PEER_EXPERTS, D_MODEL), D_MODEL ** -0.5),
        "peer_v": nrm(ks[17], (DEPTH, PEER_EXPERTS, D_MODEL), 0.5),
        "g_final": gain(ks[18], (D_MODEL,)),
    }


def reference(x_prompt, x_sample, g_mix, w_in, ret_gn, dn_conv, dn_a_log, dn_dt_bias, dn_norm,
              w_branch_a, w_branch_b, w_out, g_ffn, peer_wq, peer_k1, peer_k2, peer_u, peer_v, g_final):
    y_prompt = encoder(x_prompt, g_mix, w_in, ret_gn, dn_conv, dn_a_log, dn_dt_bias, dn_norm, w_branch_a,
                       w_branch_b, w_out, g_ffn, peer_wq, peer_k1, peer_k2, peer_u, peer_v, g_final)
    y_sample = encoder(x_sample, g_mix, w_in, ret_gn, dn_conv, dn_a_log, dn_dt_bias, dn_norm, w_branch_a,
                       w_branch_b, w_out, g_ffn, peer_wq, peer_k1, peer_k2, peer_u, peer_v, g_final)
    return (y_prompt, y_sample)
```

```python
import functools

import jax
import jax.numpy as jnp
import numpy as np
from jax import lax
from jax.experimental import pallas as pl
from jax.experimental.pallas import tpu as pltpu

F32 = jnp.float32
BF16 = jnp.bfloat16
HIGHEST = lax.Precision.HIGHEST

D = 1024
EPS = 1e-6
NH = 4
HD = 128
HW = NH * HD
ROPE_BASE = 10000.0
DN_CONV = 5
DN_C = 64
RET_C = 512
P_HEADS = 8
P_DK = 256
P_NKEYS = 128
P_TOPK = 16
P_NSEL = P_HEADS * P_TOPK
LANES = 128
SUB = 4
VMEM_LIMIT = 56 * 1024 * 1024

C_DNQKV = 0
C_RET = 1536
C_Z = 3584
C_MA = 4096
C_MB = 5120
C_SMALL = 6144
PW = 6272
PCH = 896


def _dot(a, b, **kw):
    return jnp.dot(a, b, preferred_element_type=F32, **kw)


def _dot_nt(a, b, **kw):
    return lax.dot_general(a, b, (((1,), (1,)), ((), ())), preferred_element_type=F32, **kw)


def _params(sem=None):
    return pltpu.CompilerParams(dimension_semantics=sem, vmem_limit_bytes=VMEM_LIMIT)


def _in_proj_kernel(x_ref, g_ref, w_ref, p_ref):
    x = x_ref[...]
    h = x * lax.rsqrt(jnp.mean(x * x, axis=-1, keepdims=True) + EPS) * g_ref[...]
    h = h.astype(BF16)
    for c in range(0, PW, PCH):
        p_ref[:, c:c + PCH] = _dot(h, w_ref[:, c:c + PCH])


def _in_proj(x2, g, w_cat, tm=256):
    n = x2.shape[0]
    return pl.pallas_call(
        _in_proj_kernel,
        grid=(n // tm,),
        in_specs=[pl.BlockSpec((tm, D), lambda i: (i, 0)),
                  pl.BlockSpec((1, D), lambda i: (0, 0)),
                  pl.BlockSpec((D, PW), lambda i: (0, 0))],
        out_specs=pl.BlockSpec((tm, PW), lambda i: (i, 0)),
        out_shape=jax.ShapeDtypeStruct((n, PW), F32),
        compiler_params=_params(("parallel",)),
        name="in_proj",
    )(x2, g.reshape(1, D), w_cat)


def _ret_kernel(q_ref, k_ref, v_ref, gt_ref, cos_ref, sin_ref, dmat_ref, vec_ref, cdec_ref, gn_ref,
                o_ref, run_ref, stash_ref):
    ph = pl.program_id(1)
    c = pl.program_id(2)
    nc = pl.num_programs(2)

    @pl.when(c == 0)
    def _():
        run_ref[...] = jnp.zeros_like(run_ref)

    cos2 = cos_ref[...]
    sinm = sin_ref[...]

    def rot(t):
        return t * cos2 + pltpu.roll(t, HD // 2, axis=1) * sinm

    def state_update(h, kh, vh, zeta, dec_row):
        kz = (kh * zeta).T.astype(BF16)
        contrib = _dot(kz, vh.astype(BF16))
        run_ref[h] = run_ref[h] * cdec_ref[dec_row:dec_row + 1, :] + contrib

    @pl.when(ph == 0)
    def _():
        ci = nc - 1 - c
        for h in range(NH):
            sl = slice(h * HD, (h + 1) * HD)
            kh = rot(k_ref[0, :, sl])
            stash_ref[ci, h] = run_ref[h]
            state_update(h, kh, v_ref[0, :, sl], vec_ref[3, h], NH + h)

    @pl.when(ph == 1)
    def _():
        for h in range(NH):
            sl = slice(h * HD, (h + 1) * HD)
            qh = rot(q_ref[0, :, sl]) * (HD ** -0.5)
            kh = rot(k_ref[0, :, sl])
            vh = v_ref[0, :, sl]
            vb = vh.astype(BF16)
            scores = _dot_nt(qh.astype(BF16), kh.astype(BF16)) * dmat_ref[h]
            o = _dot(scores.astype(BF16), vb)
            o += _dot((qh * vec_ref[0, h]).astype(BF16), run_ref[h].astype(BF16))
            o += _dot((qh * vec_ref[1, h]).astype(BF16), stash_ref[c, h].astype(BF16))
            state_update(h, kh, vh, vec_ref[2, h], h)
            mu = jnp.mean(o, axis=-1, keepdims=True)
            oc = o - mu
            var = jnp.mean(oc * oc, axis=-1, keepdims=True)
            on = oc * lax.rsqrt(var + EPS) * gn_ref[:, sl]
            gate = gt_ref[0, :, sl]
            o_ref[0, :, sl] = (on * (gate * jax.nn.sigmoid(gate))).astype(o_ref.dtype)


def _ret_tables(s):
    half = HD // 2
    inv_freq = 1.0 / (ROPE_BASE ** jnp.linspace(0.0, 1.0, half, dtype=F32))
    ang = jnp.arange(s, dtype=F32)[:, None] * inv_freq[None, :]
    cos, sin = jnp.cos(ang), jnp.sin(ang)
    cos2 = jnp.concatenate([cos, cos], axis=-1)
    sinm = jnp.concatenate([-sin, sin], axis=-1)
    lg = jnp.log(1.0 - 2.0 ** (-5.0 - jnp.arange(NH, dtype=F32)))
    lgb = lg[::-1]
    c = RET_C
    pos = jnp.arange(c, dtype=F32)
    diff = pos[:, None] - pos[None, :]
    dmat = jnp.where(diff >= 0, jnp.exp(lg[:, None, None] * jnp.maximum(diff, 0.0)),
                     jnp.exp(lgb[:, None, None] * jnp.maximum(-diff, 0.0)))
    xi_f = jnp.exp(lg[:, None] * (pos + 1.0))
    xi_b = jnp.exp(lgb[:, None] * (c - pos))
    zeta_f = jnp.exp(lg[:, None] * (c - 1 - pos))
    zeta_b = jnp.exp(lgb[:, None] * pos)
    vec = jnp.broadcast_to(jnp.stack([xi_f, xi_b, zeta_f, zeta_b])[..., None], (4, NH, c, LANES))
    cdec = jnp.concatenate([jnp.exp(lg * c), jnp.exp(lgb * c)])
    cdec = jnp.broadcast_to(cdec[:, None], (2 * NH, LANES))
    return cos2, sinm, dmat, vec, cdec


def _retention(p3, ret_gn):
    b, s, _ = p3.shape
    c = RET_C
    nc = s // c
    cos2, sinm, dmat, vec, cdec = _ret_tables(s)

    def chunk(ph, ci):
        return jnp.where(ph == 0, nc - 1 - ci, ci)

    def pspec(col):
        return pl.BlockSpec((1, c, HW), lambda bi, ph, ci: (bi, chunk(ph, ci), col))

    base = C_RET // HW
    return pl.pallas_call(
        _ret_kernel,
        grid=(b, 2, nc),
        in_specs=[pspec(base), pspec(base + 1), pspec(base + 2), pspec(base + 3),
                  pl.BlockSpec((c, HD), lambda bi, ph, ci: (chunk(ph, ci), 0)),
                  pl.BlockSpec((c, HD), lambda bi, ph, ci: (chunk(ph, ci), 0)),
                  pl.BlockSpec((NH, c, c), lambda bi, ph, ci: (0, 0, 0)),
                  pl.BlockSpec((4, NH, c, LANES), lambda bi, ph, ci: (0, 0, 0, 0)),
                  pl.BlockSpec((2 * NH, LANES), lambda bi, ph, ci: (0, 0)),
                  pl.BlockSpec((1, HW), lambda bi, ph, ci: (0, 0))],
        out_specs=pl.BlockSpec((1, c, HW), lambda bi, ph, ci: (bi, ci * ph, 0)),
        out_shape=jax.ShapeDtypeStruct((b, s, HW), BF16),
        scratch_shapes=[pltpu.VMEM((NH, HD, HD), F32), pltpu.VMEM((nc, NH, HD, HD), F32)],
        compiler_params=_params(("parallel", "arbitrary", "arbitrary")),
        name="retention",
    )(p3, p3, p3, p3, cos2, sinm, dmat, vec, cdec, ret_gn.reshape(1, HW))


def _gdn_prep_kernel(prev_ref, cur_ref, next_ref, sm_ref, cw_ref, alog_ref, dtb_ref, qkv_ref, smo_ref, *, tc):
    i = pl.program_id(1)
    last = pl.num_programs(1) - 1
    prev = jnp.where(i == 0, 0.0, prev_ref[0])
    nxt = jnp.where(i == last, 0.0, next_ref[0])
    ext = jnp.concatenate([prev, cur_ref[0], nxt], axis=0)
    nrow = tc + 16
    half = DN_CONV // 2
    y = jnp.zeros((tc, 3 * HW), F32)
    for j in range(DN_CONV):
        shifted = ext if j == half else pltpu.roll(ext, (half - j) % nrow, axis=0)
        y = y + shifted[8:8 + tc] * cw_ref[j:j + 1, :]
    y = y * jax.nn.sigmoid(y)
    for h in range(2 * NH):
        sl = slice(h * HD, (h + 1) * HD)
        t = y[:, sl]
        t = t * lax.rsqrt(jnp.sum(t * t, axis=-1, keepdims=True) + EPS)
        if h < NH:
            t = t * (HD ** -0.5)
        qkv_ref[0, :, sl] = t
    qkv_ref[0, :, 2 * HW:] = y[:, 2 * HW:]
    a = sm_ref[0]
    lane = lax.broadcasted_iota(jnp.int32, a.shape, 1)
    z = a + dtb_ref[...]
    softplus = jnp.maximum(z, 0.0) + jnp.log1p(jnp.exp(-jnp.abs(z)))
    g = -jnp.exp(alog_ref[...]) * softplus
    smo_ref[0] = jnp.where(lane < 2 * NH, jax.nn.sigmoid(a), g)


def _gdn_prep(p3, conv_w, a_log, dt_bias, tc=512):
    b, s, _ = p3.shape
    nt = s // tc
    r8 = tc // 8
    w3 = 3 * HW
    zeros8 = jnp.zeros((2 * NH,), F32)
    pad = jnp.zeros((LANES - 4 * NH,), F32)
    alog = jnp.concatenate([zeros8, a_log[0], a_log[1], pad]).reshape(1, LANES)
    dtb = jnp.concatenate([zeros8, dt_bias[0], dt_bias[1], pad]).reshape(1, LANES)
    return pl.pallas_call(
        functools.partial(_gdn_prep_kernel, tc=tc),
        grid=(b, nt),
        in_specs=[pl.BlockSpec((1, 8, w3), lambda bi, i: (bi, jnp.maximum(i * r8 - 1, 0), 0)),
                  pl.BlockSpec((1, tc, w3), lambda bi, i: (bi, i, 0)),
                  pl.BlockSpec((1, 8, w3), lambda bi, i: (bi, jnp.minimum((i + 1) * r8, s // 8 - 1), 0)),
                  pl.BlockSpec((1, tc, LANES), lambda bi, i: (bi, i, C_SMALL // LANES)),
                  pl.BlockSpec((DN_CONV, w3), lambda bi, i: (0, 0)),
                  pl.BlockSpec((1, LANES), lambda bi, i: (0, 0)),
                  pl.BlockSpec((1, LANES), lambda bi, i: (0, 0))],
        out_specs=[pl.BlockSpec((1, tc, w3), lambda bi, i: (bi, i, 0)),
                   pl.BlockSpec((1, tc, LANES), lambda bi, i: (bi, i, 0))],
        out_shape=[jax.ShapeDtypeStruct((b, s, w3), F32), jax.ShapeDtypeStruct((b, s, LANES), F32)],
        compiler_params=_params(("parallel", "parallel")),
        name="gdn_prep",
    )(p3, p3, p3, p3, conv_w, alog, dtb)


def _gdn_chunk(qn, kn, vv, gcum, beta, s_ref, rev):
    c = DN_C
    row = lax.broadcasted_iota(jnp.int32, (c, c), 0)
    col = lax.broadcasted_iota(jnp.int32, (c, c), 1)
    incl = (row <= col) if rev else (row >= col)
    strict = (row < col) if rev else (row > col)
    lane = lax.broadcasted_iota(jnp.int32, (c, LANES), 1)
    xa = jnp.where(lane == 0, gcum, jnp.where(lane == 1, 1.0, 0.0))
    ya = jnp.where(lane == 0, 1.0, jnp.where(lane == 1, -gcum, 0.0))
    diff = _dot_nt(xa, ya, precision=HIGHEST)
    lmat = jnp.where(incl, jnp.exp(jnp.where(incl, diff, 0.0)), 0.0)
    kb = kn * beta
    knb = kn.astype(BF16)
    a = jnp.where(strict, _dot_nt(kb.astype(BF16), knb) * lmat, 0.0)
    eye = jnp.where(row == col, 1.0, 0.0)
    tinv = eye - a
    apow = _dot(a, a, precision=HIGHEST)
    steps = int(np.log2(c)) - 1
    for it in range(steps):
        tinv = tinv + _dot(tinv, apow, precision=HIGHEST)
        if it + 1 < steps:
            apow = _dot(apow, apow, precision=HIGHEST)
    eg = jnp.exp(gcum)
    u = _dot(tinv, vv * beta, precision=HIGHEST)
    w = _dot(tinv, kb * eg, precision=HIGHEST)
    attn = _dot_nt(qn.astype(BF16), knb) * lmat
    gl = gcum[0:1, :] if rev else gcum[c - 1:c, :]
    q_dec = qn * eg
    k_dec = kn * jnp.exp(gl - gcum)
    g_tot = jnp.exp(gl)
    s = s_ref[...]
    sb = s.astype(BF16)
    v_new = u - _dot(w.astype(BF16), sb)
    vnb = v_new.astype(BF16)
    o = _dot(q_dec.astype(BF16), sb) + _dot(attn.astype(BF16), vnb)
    s_ref[...] = s * g_tot + _dot(k_dec.T.astype(BF16), vnb)
    return o


def _gdn_scan_kernel(qkvf_ref, smf_ref, qkvb_ref, smb_ref, of_ref, ob_ref, s_ref):
    @pl.when(pl.program_id(1) == 0)
    def _():
        s_ref[...] = jnp.zeros_like(s_ref)

    c = DN_C
    row = lax.broadcasted_iota(jnp.int32, (c, c), 0)
    col = lax.broadcasted_iota(jnp.int32, (c, c), 1)
    lane = lax.broadcasted_iota(jnp.int32, (c, LANES), 1)
    for d, (qkv_ref, sm_ref, o_ref) in enumerate(((qkvf_ref, smf_ref, of_ref), (qkvb_ref, smb_ref, ob_ref))):
        tri = jnp.where((row <= col) if d else (row >= col), 1.0, 0.0)
        sm = sm_ref[0]
        gc_all = _dot(tri, sm, precision=HIGHEST)
        for h in range(NH):
            sl = slice(h * HD, (h + 1) * HD)
            beta = jnp.sum(jnp.where(lane == NH * d + h, sm, 0.0), axis=-1, keepdims=True)
            gcum = jnp.sum(jnp.where(lane == 2 * NH + NH * d + h, gc_all, 0.0), axis=-1, keepdims=True)
            o = _gdn_chunk(qkv_ref[0, :, sl], qkv_ref[0, :, HW + h * HD:HW + (h + 1) * HD],
                           qkv_ref[0, :, 2 * HW + h * HD:2 * HW + (h + 1) * HD],
                           gcum, beta, s_ref.at[d, h], bool(d))
            o_ref[0, :, sl] = o


def _gdn_scan(qkv, sm):
    b, s, w3 = qkv.shape
    c = DN_C
    nc = s // c
    return pl.pallas_call(
        _gdn_scan_kernel,
        grid=(b, nc),
        in_specs=[pl.BlockSpec((1, c, w3), lambda bi, i: (bi, i, 0)),
                  pl.BlockSpec((1, c, LANES), lambda bi, i: (bi, i, 0)),
                  pl.BlockSpec((1, c, w3), lambda bi, i: (bi, nc - 1 - i, 0)),
                  pl.BlockSpec((1, c, LANES), lambda bi, i: (bi, nc - 1 - i, 0))],
        out_specs=[pl.BlockSpec((1, c, HW), lambda bi, i: (bi, i, 0)),
                   pl.BlockSpec((1, c, HW), lambda bi, i: (bi, nc - 1 - i, 0))],
        out_shape=[jax.ShapeDtypeStruct((b, s, HW), F32), jax.ShapeDtypeStruct((b, s, HW), F32)],
        scratch_shapes=[pltpu.VMEM((2, NH, HD, HD), F32)],
        compiler_params=_params(("parallel", "arbitrary")),
        name="gdn_scan",
    )(qkv, sm, qkv, sm)


def _merge_kernel(x_ref, oa_ref, of_ref, ob_ref, z_ref, ma_ref, mb_ref, ng_ref, wa_ref, wb_ref, wo_ref, x1_ref):
    ob_parts = []
    for h in range(NH):
        sl = slice(h * HD, (h + 1) * HD)
        o = of_ref[:, sl] + ob_ref[:, sl]
        o = o * lax.rsqrt(jnp.mean(o * o, axis=-1, keepdims=True) + EPS) * ng_ref[...]
        z = z_ref[:, sl]
        ob_parts.append((o * (z * jax.nn.sigmoid(z))).astype(BF16))
    ob = jnp.concatenate(ob_parts, axis=-1)
    ya = _dot(oa_ref[...], wa_ref[...])
    yb = _dot(ob, wb_ref[...])
    merged = jax.nn.sigmoid(ma_ref[...]) * ya + jax.nn.sigmoid(mb_ref[...]) * yb
    x1_ref[...] = x_ref[...] + _dot(merged.astype(BF16), wo_ref[...])


def _merge(x2, oa, o_f, o_b, p2, norm_g, wa, wb, wo, tm=512):
    n = x2.shape[0]
    const = lambda i: (0, 0)
    return pl.pallas_call(
        _merge_kernel,
        grid=(n // tm,),
        in_specs=[pl.BlockSpec((tm, D), lambda i: (i, 0)),
                  pl.BlockSpec((tm, HW), lambda i: (i, 0)),
                  pl.BlockSpec((tm, HW), lambda i: (i, 0)),
                  pl.BlockSpec((tm, HW), lambda i: (i, 0)),
                  pl.BlockSpec((tm, HW), lambda i: (i, C_Z // HW)),
                  pl.BlockSpec((tm, D), lambda i: (i, C_MA // D)),
                  pl.BlockSpec((tm, D), lambda i: (i, C_MB // D)),
                  pl.BlockSpec((1, HD), const),
                  pl.BlockSpec((HW, D), const),
                  pl.BlockSpec((HW, D), const),
                  pl.BlockSpec((D, D), const)],
        out_specs=pl.BlockSpec((tm, D), lambda i: (i, 0)),
        out_shape=jax.ShapeDtypeStruct((n, D), F32),
        compiler_params=_params(("parallel",)),
        name="merge",
    )(x2, oa, o_f, o_b, p2, p2, p2, norm_g.reshape(1, HD), wa, wb, wo)


def _top_rows(s, payload=None):
    r = s.shape[0]
    row = lax.broadcasted_iota(jnp.int32, s.shape, 0)
    vals, idxs = [], []
    for _ in range(P_TOPK):
        m = jnp.max(s, axis=0, keepdims=True)
        first = jnp.min(jnp.where(s == m, row, r), axis=0, keepdims=True)
        sel = row == first
        vals.append(m)
        idxs.append(first if payload is None else jnp.max(jnp.where(sel, payload, -1), axis=0, keepdims=True))
        s = jnp.where(sel, -jnp.inf, s)
    return vals, idxs


def _stack_rows(rows, dtype):
    t = rows[0].shape[1]
    row = lax.broadcasted_iota(jnp.int32, (len(rows), t), 0)
    out = jnp.zeros((len(rows), t), dtype)
    for i, r in enumerate(rows):
        out = jnp.where(row == i, r, out)
    return out


def _route_kernel(x1_ref, g_ref, wq_ref, k1_ref, k2_ref, h2_ref, eidx_ref, gate_ref):
    x = x1_ref[...]
    h2 = x * lax.rsqrt(jnp.mean(x * x, axis=-1, keepdims=True) + EPS) * g_ref[...]
    h2_ref[...] = h2
    qp = _dot(h2.astype(BF16), wq_ref[...])
    gates, eidxs = [], []
    for h in range(P_HEADS):
        q1 = qp[:, h * P_DK:h * P_DK + P_DK // 2].astype(BF16)
        q2 = qp[:, h * P_DK + P_DK // 2:(h + 1) * P_DK].astype(BF16)
        s1 = _dot_nt(k1_ref[h], q1)
        s2 = _dot_nt(k2_ref[h], q2)
        v1, i1 = _top_rows(s1)
        v2, i2 = _top_rows(s2)
        v2s = _stack_rows(v2, F32)
        i2s = _stack_rows(i2, jnp.int32)
        cand = jnp.concatenate([v1[a] + v2s for a in range(P_TOPK)], axis=0)
        cidx = jnp.concatenate([i1[a] * P_NKEYS + i2s for a in range(P_TOPK)], axis=0)
        sc, ei = _top_rows(cand, cidx)
        scs = _stack_rows(sc, F32)
        ex = jnp.exp(scs - sc[0])
        gates.append(ex / jnp.sum(ex, axis=0, keepdims=True))
        eidxs.append(_stack_rows(ei, jnp.int32))
    gate_ref[...] = jnp.concatenate(gates, axis=0).T
    eidx_ref[...] = jnp.concatenate(eidxs, axis=0).T


def _route(x1, g_ffn, wq, k1, k2, tm=128):
    n = x1.shape[0]
    return pl.pallas_call(
        _route_kernel,
        grid=(n // tm,),
        in_specs=[pl.BlockSpec((tm, D), lambda i: (i, 0)),
                  pl.BlockSpec((1, D), lambda i: (0, 0)),
                  pl.BlockSpec((D, P_HEADS * P_DK), lambda i: (0, 0)),
                  pl.BlockSpec((P_HEADS, P_NKEYS, P_DK // 2), lambda i: (0, 0, 0)),
                  pl.BlockSpec((P_HEADS, P_NKEYS, P_DK // 2), lambda i: (0, 0, 0))],
        out_specs=[pl.BlockSpec((tm, D), lambda i: (i, 0)),
                   pl.BlockSpec((tm, P_NSEL), lambda i: (i, 0)),
                   pl.BlockSpec((tm, P_NSEL), lambda i: (i, 0))],
        out_shape=[jax.ShapeDtypeStruct((n, D), F32),
                   jax.ShapeDtypeStruct((n, P_NSEL), jnp.int32),
                   jax.ShapeDtypeStruct((n, P_NSEL), F32)],
        compiler_params=_params(("parallel",)),
        name="route",
    )(x1, g_ffn.reshape(1, D), wq, k1, k2)


def _pack_table(t):
    bits = lax.bitcast_convert_type(t.astype(BF16), jnp.uint16).astype(jnp.uint32)
    packed = bits[:, :D // 2] | (bits[:, D // 2:] << 16)
    return lax.bitcast_convert_type(packed, jnp.int32).reshape(t.shape[0], SUB, LANES)


def _unpack(row):
    lo = pltpu.bitcast(row << 16, F32)
    hi = pltpu.bitcast(row & jnp.int32(-65536), F32)
    return lo, hi


def _peer_u_kernel(eidx_ref, x_ref, gate_ref, tab_ref, w_ref, q_ref, *, tm):
    ones = jnp.ones((8, LANES), F32)

    def body(t, carry):
        x = x_ref[t]
        xlo, xhi = x[0:SUB], x[SUB:]
        for k in range(P_NSEL):
            lo, hi = _unpack(tab_ref[eidx_ref[t, k]])
            prod = lo * xlo + hi * xhi
            q_ref[k:k + 1, :] = jnp.sum(prod, axis=0, keepdims=True)
        dots = _dot_nt(ones, q_ref[...], precision=HIGHEST)
        w_ref[pl.ds(t, 1), :] = dots[0:1]
        return carry

    lax.fori_loop(0, tm, body, 0)
    pre = w_ref[...]
    act = 0.5 * pre * (1.0 + lax.erf(pre * (2.0 ** -0.5)))
    w_ref[...] = gate_ref[...] * act


def _peer_u(eidx, h2r, gate, tab, tm=256):
    n = eidx.shape[0]
    return pl.pallas_call(
        functools.partial(_peer_u_kernel, tm=tm),
        grid=(n // tm,),
        in_specs=[pl.BlockSpec((tm, P_NSEL), lambda i: (i, 0), memory_space=pltpu.SMEM),
                  pl.BlockSpec((tm, 2 * SUB, LANES), lambda i: (i, 0, 0)),
                  pl.BlockSpec((tm, P_NSEL), lambda i: (i, 0)),
                  pl.BlockSpec(memory_space=pltpu.VMEM)],
        out_specs=pl.BlockSpec((tm, P_NSEL), lambda i: (i, 0)),
        out_shape=jax.ShapeDtypeStruct((n, P_NSEL), F32),
        scratch_shapes=[pltpu.VMEM((P_NSEL, LANES), F32)],
        compiler_params=_params(("arbitrary",)),
        name="peer_u",
    )(eidx, h2r, gate, tab)


def _peer_v_kernel(eidx_ref, w_ref, tab_ref, out_ref, *, tm):
    def body(t, carry):
        acc_lo = [jnp.zeros((SUB, LANES), F32) for _ in range(2)]
        acc_hi = [jnp.zeros((SUB, LANES), F32) for _ in range(2)]
        for k in range(P_NSEL):
            w = w_ref[t, k]
            lo, hi = _unpack(tab_ref[eidx_ref[t, k]])
            acc_lo[k % 2] = acc_lo[k % 2] + w * lo
            acc_hi[k % 2] = acc_hi[k % 2] + w * hi
        out_ref[t, 0:SUB, :] = acc_lo[0] + acc_lo[1]
        out_ref[t, SUB:, :] = acc_hi[0] + acc_hi[1]
        return carry

    lax.fori_loop(0, tm, body, 0)


def _peer_v(eidx, w, tab, tm=256):
    n = eidx.shape[0]
    return pl.pallas_call(
        functools.partial(_peer_v_kernel, tm=tm),
        grid=(n // tm,),
        in_specs=[pl.BlockSpec((tm, P_NSEL), lambda i: (i, 0), memory_space=pltpu.SMEM),
                  pl.BlockSpec((tm, P_NSEL), lambda i: (i, 0), memory_space=pltpu.SMEM),
                  pl.BlockSpec(memory_space=pltpu.VMEM)],
        out_specs=pl.BlockSpec((tm, 2 * SUB, LANES), lambda i: (i, 0, 0)),
        out_shape=jax.ShapeDtypeStruct((n, 2 * SUB, LANES), F32),
        compiler_params=_params(("arbitrary",)),
        name="peer_v",
    )(eidx, w, tab)


def _final_kernel(x1_ref, po_ref, g_ref, y_ref):
    x = x1_ref[...] + po_ref[...]
    y_ref[...] = x * lax.rsqrt(jnp.mean(x * x, axis=-1, keepdims=True) + EPS) * g_ref[...]


def _final(x1, po, g, tm=512):
    n = x1.shape[0]
    return pl.pallas_call(
        _final_kernel,
        grid=(n // tm,),
        in_specs=[pl.BlockSpec((tm, D), lambda i: (i, 0)),
                  pl.BlockSpec((tm, D), lambda i: (i, 0)),
                  pl.BlockSpec((1, D), lambda i: (0, 0))],
        out_specs=pl.BlockSpec((tm, D), lambda i: (i, 0)),
        out_shape=jax.ShapeDtypeStruct((n, D), F32),
        compiler_params=_params(("parallel",)),
        name="final",
    )(x1, po, g.reshape(1, D))


def _pack_w_in(w_in):
    w = jnp.concatenate([w_in[:, 2048:3584], w_in[:, :2048], w_in[:, 3584:4096], w_in[:, 4112:6160],
                         w_in[:, 4096:4112], jnp.zeros((D, PW - C_SMALL - 16), w_in.dtype)], axis=1)
    return w.astype(BF16)


def _encoder(x, wts):
    b, s, _ = x.shape
    n = b * s
    x2 = x.reshape(n, D)
    p2 = _in_proj(x2, wts["g_mix"], wts["w_cat"])
    p3 = p2.reshape(b, s, PW)
    oa = _retention(p3, wts["ret_gn"]).reshape(n, HW)
    qkv, sm = _gdn_prep(p3, wts["dn_conv"], wts["dn_a_log"], wts["dn_dt_bias"])
    o_f, o_b = _gdn_scan(qkv, sm)
    x1 = _merge(x2, oa, o_f.reshape(n, HW), o_b.reshape(n, HW), p2, wts["dn_norm"],
                wts["wa"], wts["wb"], wts["wo"])
    h2, eidx, gate = _route(x1, wts["g_ffn"], wts["wq"], wts["k1"], wts["k2"])
    w = _peer_u(eidx, h2.reshape(n, 2 * SUB, LANES), gate, wts["tab_u"])
    po = _peer_v(eidx, w, wts["tab_v"]).reshape(n, D)
    return _final(x1, po, wts["g_final"]).reshape(b, s, D)


def kernel(x_prompt, x_sample, g_mix, w_in, ret_gn, dn_conv, dn_a_log, dn_dt_bias, dn_norm, w_branch_a, w_branch_b, w_out, g_ffn, peer_wq, peer_k1, peer_k2, peer_u, peer_v, g_final):
    wts = dict(
        g_mix=g_mix[0], w_cat=_pack_w_in(w_in[0]), ret_gn=ret_gn[0], dn_conv=dn_conv[0],
        dn_a_log=dn_a_log[0], dn_dt_bias=dn_dt_bias[0], dn_norm=dn_norm[0],
        wa=w_branch_a[0].astype(BF16), wb=w_branch_b[0].astype(BF16), wo=w_out[0].astype(BF16),
        g_ffn=g_ffn[0], wq=peer_wq[0].astype(BF16), k1=peer_k1[0].astype(BF16), k2=peer_k2[0].astype(BF16),
        tab_u=_pack_table(peer_u[0]), tab_v=_pack_table(peer_v[0]), g_final=g_final)
    return (_encoder(x_prompt, wts), _encoder(x_sample, wts))
```

```python
import functools

import jax
import jax.numpy as jnp
import numpy as np
from jax import lax
from jax.experimental import pallas as pl
from jax.experimental.pallas import tpu as pltpu

F32 = jnp.float32
BF16 = jnp.bfloat16
HIGHEST = lax.Precision.HIGHEST

D = 1024
EPS = 1e-6
NH = 4
HD = 128
HW = NH * HD
ROPE_BASE = 10000.0
DN_CONV = 5
DN_C = 64
RET_C = 512
P_HEADS = 8
P_DK = 256
P_NKEYS = 128
P_TOPK = 16
P_NSEL = P_HEADS * P_TOPK
LANES = 128
SUB = 4
VMEM_LIMIT = 56 * 1024 * 1024

C_DNQKV = 0
C_RET = 1536
C_Z = 3584
C_MA = 4096
C_MB = 5120
C_SMALL = 6144
PW = 6272
PCH = 896


def _dot(a, b, **kw):
    return jnp.dot(a, b, preferred_element_type=F32, **kw)


def _dot_nt(a, b, **kw):
    return lax.dot_general(a, b, (((1,), (1,)), ((), ())), preferred_element_type=F32, **kw)


def _params(sem=None):
    return pltpu.CompilerParams(dimension_semantics=sem, vmem_limit_bytes=VMEM_LIMIT)


def _in_proj_kernel(x_ref, g_ref, w_ref, p_ref):
    x = x_ref[...]
    h = x * lax.rsqrt(jnp.mean(x * x, axis=-1, keepdims=True) + EPS) * g_ref[...]
    h = h.astype(BF16)
    for c in range(0, PW, PCH):
        p_ref[:, c:c + PCH] = _dot(h, w_ref[:, c:c + PCH])


def _in_proj(x2, g, w_cat, tm=256):
    n = x2.shape[0]
    return pl.pallas_call(
        _in_proj_kernel,
        grid=(n // tm,),
        in_specs=[pl.BlockSpec((tm, D), lambda i: (i, 0)),
                  pl.BlockSpec((1, D), lambda i: (0, 0)),
                  pl.BlockSpec((D, PW), lambda i: (0, 0))],
        out_specs=pl.BlockSpec((tm, PW), lambda i: (i, 0)),
        out_shape=jax.ShapeDtypeStruct((n, PW), F32),
        compiler_params=_params(("parallel",)),
        name="in_proj",
    )(x2, g.reshape(1, D), w_cat)


def _ret_kernel(q_ref, k_ref, v_ref, gt_ref, cos_ref, sin_ref, dmat_ref, vec_ref, cdec_ref, gn_ref,
                o_ref, run_ref, stash_ref):
    ph = pl.program_id(1)
    c = pl.program_id(2)
    nc = pl.num_programs(2)

    @pl.when(c == 0)
    def _():
        run_ref[...] = jnp.zeros_like(run_ref)

    cos2 = cos_ref[...]
    sinm = sin_ref[...]

    def rot(t):
        return t * cos2 + pltpu.roll(t, HD // 2, axis=1) * sinm

    def state_update(h, kh, vh, zeta, dec_row):
        kz = (kh * zeta).T.astype(BF16)
        contrib = _dot(kz, vh.astype(BF16))
        run_ref[h] = run_ref[h] * cdec_ref[dec_row:dec_row + 1, :] + contrib

    @pl.when(ph == 0)
    def _():
        ci = nc - 1 - c
        for h in range(NH):
            sl = slice(h * HD, (h + 1) * HD)
            kh = rot(k_ref[0, :, sl])
            stash_ref[ci, h] = run_ref[h]
            state_update(h, kh, v_ref[0, :, sl], vec_ref[3, h], NH + h)

    @pl.when(ph == 1)
    def _():
        for h in range(NH):
            sl = slice(h * HD, (h + 1) * HD)
            qh = rot(q_ref[0, :, sl]) * (HD ** -0.5)
            kh = rot(k_ref[0, :, sl])
            vh = v_ref[0, :, sl]
            vb = vh.astype(BF16)
            scores = _dot_nt(qh.astype(BF16), kh.astype(BF16)) * dmat_ref[h]
            o = _dot(scores.astype(BF16), vb)
            o += _dot((qh * vec_ref[0, h]).astype(BF16), run_ref[h].astype(BF16))
            o += _dot((qh * vec_ref[1, h]).astype(BF16), stash_ref[c, h].astype(BF16))
            state_update(h, kh, vh, vec_ref[2, h], h)
            mu = jnp.mean(o, axis=-1, keepdims=True)
            oc = o - mu
            var = jnp.mean(oc * oc, axis=-1, keepdims=True)
            on = oc * lax.rsqrt(var + EPS) * gn_ref[:, sl]
            gate = gt_ref[0, :, sl]
            o_ref[0, :, sl] = (on * (gate * jax.nn.sigmoid(gate))).astype(o_ref.dtype)


def _ret_tables(s):
    half = HD // 2
    inv_freq = 1.0 / (ROPE_BASE ** jnp.linspace(0.0, 1.0, half, dtype=F32))
    ang = jnp.arange(s, dtype=F32)[:, None] * inv_freq[None, :]
    cos, sin = jnp.cos(ang), jnp.sin(ang)
    cos2 = jnp.concatenate([cos, cos], axis=-1)
    sinm = jnp.concatenate([-sin, sin], axis=-1)
    lg = jnp.log(1.0 - 2.0 ** (-5.0 - jnp.arange(NH, dtype=F32)))
    lgb = lg[::-1]
    c = RET_C
    pos = jnp.arange(c, dtype=F32)
    diff = pos[:, None] - pos[None, :]
    dmat = jnp.where(diff >= 0, jnp.exp(lg[:, None, None] * jnp.maximum(diff, 0.0)),
                     jnp.exp(lgb[:, None, None] * jnp.maximum(-diff, 0.0)))
    xi_f = jnp.exp(lg[:, None] * (pos + 1.0))
    xi_b = jnp.exp(lgb[:, None] * (c - pos))
    zeta_f = jnp.exp(lg[:, None] * (c - 1 - pos))
    zeta_b = jnp.exp(lgb[:, None] * pos)
    vec = jnp.broadcast_to(jnp.stack([xi_f, xi_b, zeta_f, zeta_b])[..., None], (4, NH, c, LANES))
    cdec = jnp.concatenate([jnp.exp(lg * c), jnp.exp(lgb * c)])
    cdec = jnp.broadcast_to(cdec[:, None], (2 * NH, LANES))
    return cos2, sinm, dmat, vec, cdec


def _retention(p3, ret_gn):
    b, s, _ = p3.shape
    c = RET_C
    nc = s // c
    cos2, sinm, dmat, vec, cdec = _ret_tables(s)

    def chunk(ph, ci):
        return jnp.where(ph == 0, nc - 1 - ci, ci)

    def pspec(col):
        return pl.BlockSpec((1, c, HW), lambda bi, ph, ci: (bi, chunk(ph, ci), col))

    base = C_RET // HW
    return pl.pallas_call(
        _ret_kernel,
        grid=(b, 2, nc),
        in_specs=[pspec(base), pspec(base + 1), pspec(base + 2), pspec(base + 3),
                  pl.BlockSpec((c, HD), lambda bi, ph, ci: (chunk(ph, ci), 0)),
                  pl.BlockSpec((c, HD), lambda bi, ph, ci: (chunk(ph, ci), 0)),
                  pl.BlockSpec((NH, c, c), lambda bi, ph, ci: (0, 0, 0)),
                  pl.BlockSpec((4, NH, c, LANES), lambda bi, ph, ci: (0, 0, 0, 0)),
                  pl.BlockSpec((2 * NH, LANES), lambda bi, ph, ci: (0, 0)),
                  pl.BlockSpec((1, HW), lambda bi, ph, ci: (0, 0))],
        out_specs=pl.BlockSpec((1, c, HW), lambda bi, ph, ci: (bi, ci * ph, 0)),
        out_shape=jax.ShapeDtypeStruct((b, s, HW), BF16),
        scratch_shapes=[pltpu.VMEM((NH, HD, HD), F32), pltpu.VMEM((nc, NH, HD, HD), F32)],
        compiler_params=_params(("parallel", "arbitrary", "arbitrary")),
        name="retention",
    )(p3, p3, p3, p3, cos2, sinm, dmat, vec, cdec, ret_gn.reshape(1, HW))


def _gdn_prep_kernel(prev_ref, cur_ref, next_ref, sm_ref, cw_ref, alog_ref, dtb_ref, qkv_ref, smo_ref, *, tc):
    i = pl.program_id(1)
    last = pl.num_programs(1) - 1
    prev = jnp.where(i == 0, 0.0, prev_ref[0])
    nxt = jnp.where(i == last, 0.0, next_ref[0])
    ext = jnp.concatenate([prev, cur_ref[0], nxt], axis=0)
    nrow = tc + 16
    half = DN_CONV // 2
    y = jnp.zeros((tc, 3 * HW), F32)
    for j in range(DN_CONV):
        shifted = ext if j == half else pltpu.roll(ext, (half - j) % nrow, axis=0)
        y = y + shifted[8:8 + tc] * cw_ref[j:j + 1, :]
    y = y * jax.nn.sigmoid(y)
    for h in range(2 * NH):
        sl = slice(h * HD, (h + 1) * HD)
        t = y[:, sl]
        t = t * lax.rsqrt(jnp.sum(t * t, axis=-1, keepdims=True) + EPS)
        if h < NH:
            t = t * (HD ** -0.5)
        qkv_ref[0, :, sl] = t
    qkv_ref[0, :, 2 * HW:] = y[:, 2 * HW:]
    a = sm_ref[0]
    lane = lax.broadcasted_iota(jnp.int32, a.shape, 1)
    z = a + dtb_ref[...]
    softplus = jnp.maximum(z, 0.0) + jnp.log1p(jnp.exp(-jnp.abs(z)))
    g = -jnp.exp(alog_ref[...]) * softplus
    smo_ref[0] = jnp.where(lane < 2 * NH, jax.nn.sigmoid(a), g)


def _gdn_prep(p3, conv_w, a_log, dt_bias, tc=512):
    b, s, _ = p3.shape
    nt = s // tc
    r8 = tc // 8
    w3 = 3 * HW
    zeros8 = jnp.zeros((2 * NH,), F32)
    pad = jnp.zeros((LANES - 4 * NH,), F32)
    alog = jnp.concatenate([zeros8, a_log[0], a_log[1], pad]).reshape(1, LANES)
    dtb = jnp.concatenate([zeros8, dt_bias[0], dt_bias[1], pad]).reshape(1, LANES)
    return pl.pallas_call(
        functools.partial(_gdn_prep_kernel, tc=tc),
        grid=(b, nt),
        in_specs=[pl.BlockSpec((1, 8, w3), lambda bi, i: (bi, jnp.maximum(i * r8 - 1, 0), 0)),
                  pl.BlockSpec((1, tc, w3), lambda bi, i: (bi, i, 0)),
                  pl.BlockSpec((1, 8, w3), lambda bi, i: (bi, jnp.minimum((i + 1) * r8, s // 8 - 1), 0)),
                  pl.BlockSpec((1, tc, LANES), lambda bi, i: (bi, i, C_SMALL // LANES)),
                  pl.BlockSpec((DN_CONV, w3), lambda bi, i: (0, 0)),
                  pl.BlockSpec((1, LANES), lambda bi, i: (0, 0)),
                  pl.BlockSpec((1, LANES), lambda bi, i: (0, 0))],
        out_specs=[pl.BlockSpec((1, tc, w3), lambda bi, i: (bi, i, 0)),
                   pl.BlockSpec((1, tc, LANES), lambda bi, i: (bi, i, 0))],
        out_shape=[jax.ShapeDtypeStruct((b, s, w3), F32), jax.ShapeDtypeStruct((b, s, LANES), F32)],
        compiler_params=_params(("parallel", "parallel")),
        name="gdn_prep",
    )(p3, p3, p3, p3, conv_w, alog, dtb)


def _gdn_terms_kernel(qkv_ref, sm_ref, *out_refs, nchunk):
    c = DN_C
    row = lax.broadcasted_iota(jnp.int32, (c, c), 0)
    col = lax.broadcasted_iota(jnp.int32, (c, c), 1)
    lane = lax.broadcasted_iota(jnp.int32, (c, LANES), 1)
    gt_ref = out_refs[-1]
    ids, qn, kn, vv, gcum, beta, lmat, strict = [], [], [], [], [], [], [], []
    for ci in range(nchunk):
        rows = slice(ci * c, (ci + 1) * c)
        sm = sm_ref[0, rows, :]
        for d in range(2):
            tri = jnp.where((row <= col) if d else (row >= col), 1.0, 0.0)
            gc_all = _dot(tri, sm, precision=HIGHEST)
            gc_t = gc_all.T
            incl = (row <= col) if d else (row >= col)
            for h in range(NH):
                lg = 2 * NH + NH * d + h
                ids.append((ci, d, h))
                qn.append(qkv_ref[0, rows, h * HD:(h + 1) * HD])
                kn.append(qkv_ref[0, rows, HW + h * HD:HW + (h + 1) * HD])
                vv.append(qkv_ref[0, rows, 2 * HW + h * HD:2 * HW + (h + 1) * HD])
                beta.append(jnp.sum(jnp.where(lane == NH * d + h, sm, 0.0), axis=-1, keepdims=True))
                g = jnp.sum(jnp.where(lane == lg, gc_all, 0.0), axis=-1, keepdims=True)
                gcum.append(g)
                lmat.append(jnp.where(incl, jnp.exp(jnp.where(incl, g - gc_t[lg:lg + 1, :], 0.0)), 0.0))
                strict.append((row < col) if d else (row > col))
    n = len(ids)
    rng = range(n)
    kb = [kn[i] * beta[i] for i in rng]
    knb = [kn[i].astype(BF16) for i in rng]
    kk = [_dot_nt(kb[i].astype(BF16), knb[i]) for i in rng]
    a = [jnp.where(strict[i], kk[i] * lmat[i], 0.0) for i in rng]
    ab = [a[i].astype(BF16) for i in rng]
    p = [_dot(ab[i], ab[i]) for i in rng]
    m = [-a[i] for i in rng]
    steps = int(np.log2(c)) - 1
    for it in range(steps):
        pb = [p[i].astype(BF16) for i in rng]
        mp = [_dot(m[i].astype(BF16), pb[i]) for i in rng]
        m = [m[i] + p[i] + mp[i] for i in rng]
        if it + 1 < steps:
            p = [_dot(pb[i], pb[i]) for i in rng]
    eg = [jnp.exp(gcum[i]) for i in rng]
    rhs = [jnp.concatenate([vv[i] * beta[i], kb[i] * eg[i]], axis=-1) for i in rng]
    corr = [_dot(m[i].astype(BF16), rhs[i].astype(BF16)) for i in rng]
    attn = [_dot_nt(qn[i].astype(BF16), knb[i]) for i in rng]
    for i, (ci, d, h) in enumerate(ids):
        u_ref, w_ref, qd_ref, kdt_ref, att_ref = out_refs[5 * d:5 * d + 5]
        rows = slice(ci * c, (ci + 1) * c)
        sl = slice(h * HD, (h + 1) * HD)
        sol = rhs[i] + corr[i]
        gl = gcum[i][0:1, :] if d else gcum[i][c - 1:c, :]
        u_ref[0, rows, sl] = sol[:, :HD]
        w_ref[0, rows, sl] = sol[:, HD:].astype(BF16)
        qd_ref[0, rows, sl] = (qn[i] * eg[i]).astype(BF16)
        kdt_ref[0, ci, h] = (kn[i] * jnp.exp(gl - gcum[i])).T.astype(BF16)
        att_ref[0, ci, h] = (attn[i] * lmat[i]).astype(BF16)
        gt_ref[0, ci, NH * d + h:NH * d + h + 1, :] = jnp.broadcast_to(jnp.exp(gl), (1, LANES))


def _gdn_terms(qkv, sm, nchunk=2):
    b, s, w3 = qkv.shape
    c = DN_C
    nc = s // c
    r = nchunk * c
    tok = lambda dt: (pl.BlockSpec((1, r, HW), lambda bi, i: (bi, i, 0)), jax.ShapeDtypeStruct((b, s, HW), dt))
    kdt = (pl.BlockSpec((1, nchunk, NH, HD, c), lambda bi, i: (bi, i, 0, 0, 0)),
           jax.ShapeDtypeStruct((b, nc, NH, HD, c), BF16))
    att = (pl.BlockSpec((1, nchunk, NH, c, c), lambda bi, i: (bi, i, 0, 0, 0)),
           jax.ShapeDtypeStruct((b, nc, NH, c, c), BF16))
    gt = (pl.BlockSpec((1, nchunk, 2 * NH, LANES), lambda bi, i: (bi, i, 0, 0)),
          jax.ShapeDtypeStruct((b, nc, 2 * NH, LANES), F32))
    outs = [tok(F32), tok(BF16), tok(BF16), kdt, att] * 2 + [gt]
    return pl.pallas_call(
        functools.partial(_gdn_terms_kernel, nchunk=nchunk),
        grid=(b, nc // nchunk),
        in_specs=[pl.BlockSpec((1, r, w3), lambda bi, i: (bi, i, 0)),
                  pl.BlockSpec((1, r, LANES), lambda bi, i: (bi, i, 0))],
        out_specs=[o[0] for o in outs],
        out_shape=[o[1] for o in outs],
        compiler_params=_params(("parallel", "parallel")),
        name="gdn_terms",
    )(qkv, sm)


def _gdn_scan_kernel(*refs, nb):
    (uf, wf, qdf, kdtf, attf, gtf, ub, wb, qdb, kdtb, attb, gtb, of_ref, ob_ref, s_ref) = refs

    @pl.when(pl.program_id(1) == 0)
    def _():
        s_ref[...] = jnp.zeros_like(s_ref)

    dirs = ((uf, wf, qdf, kdtf, attf, gtf, of_ref), (ub, wb, qdb, kdtb, attb, gtb, ob_ref))
    ids = [(bi, d, h) for bi in range(nb) for d in range(2) for h in range(NH)]
    sl = [slice(h * HD, (h + 1) * HD) for h in range(NH)]
    s = [s_ref[bi, d, h] for bi, d, h in ids]
    sb = [t.astype(BF16) for t in s]
    ws = [_dot(dirs[d][1][bi, :, sl[h]], sb[i]) for i, (bi, d, h) in enumerate(ids)]
    qs = [_dot(dirs[d][2][bi, :, sl[h]], sb[i]) for i, (bi, d, h) in enumerate(ids)]
    vnb = [(dirs[d][0][bi, :, sl[h]] - ws[i]).astype(BF16) for i, (bi, d, h) in enumerate(ids)]
    av = [_dot(dirs[d][4][bi, 0, h], vnb[i]) for i, (bi, d, h) in enumerate(ids)]
    kv = [_dot(dirs[d][3][bi, 0, h], vnb[i]) for i, (bi, d, h) in enumerate(ids)]
    for i, (bi, d, h) in enumerate(ids):
        dirs[d][6][bi, :, sl[h]] = qs[i] + av[i]
        gt = dirs[d][5][bi, 0, NH * d + h:NH * d + h + 1, :]
        s_ref[bi, d, h] = s[i] * gt + kv[i]


def _gdn_scan(terms, nb):
    uf, wf, qdf, kdtf, attf, ub, wb, qdb, kdtb, attb, gt = terms
    b, s, _ = uf.shape
    c = DN_C
    nc = s // c
    fwd = lambda bi, i: i
    bwd = lambda bi, i: nc - 1 - i

    def specs(pos):
        return [pl.BlockSpec((nb, c, HW), lambda bi, i: (bi, pos(bi, i), 0)),
                pl.BlockSpec((nb, c, HW), lambda bi, i: (bi, pos(bi, i), 0)),
                pl.BlockSpec((nb, c, HW), lambda bi, i: (bi, pos(bi, i), 0)),
                pl.BlockSpec((nb, 1, NH, HD, c), lambda bi, i: (bi, pos(bi, i), 0, 0, 0)),
                pl.BlockSpec((nb, 1, NH, c, c), lambda bi, i: (bi, pos(bi, i), 0, 0, 0)),
                pl.BlockSpec((nb, 1, 2 * NH, LANES), lambda bi, i: (bi, pos(bi, i), 0, 0))]

    return pl.pallas_call(
        functools.partial(_gdn_scan_kernel, nb=nb),
        grid=(b // nb, nc),
        in_specs=specs(fwd) + specs(bwd),
        out_specs=[pl.BlockSpec((nb, c, HW), lambda bi, i: (bi, i, 0)),
                   pl.BlockSpec((nb, c, HW), lambda bi, i: (bi, nc - 1 - i, 0))],
        out_shape=[jax.ShapeDtypeStruct((b, s, HW), F32), jax.ShapeDtypeStruct((b, s, HW), F32)],
        scratch_shapes=[pltpu.VMEM((nb, 2, NH, HD, HD), F32)],
        compiler_params=_params(("parallel", "arbitrary")),
        name="gdn_scan",
    )(uf, wf, qdf, kdtf, attf, gt, ub, wb, qdb, kdtb, attb, gt)


def _merge_kernel(x_ref, oa_ref, of_ref, ob_ref, z_ref, ma_ref, mb_ref, ng_ref, wa_ref, wb_ref, wo_ref, x1_ref):
    ob_parts = []
    for h in range(NH):
        sl = slice(h * HD, (h + 1) * HD)
        o = of_ref[:, sl] + ob_ref[:, sl]
        o = o * lax.rsqrt(jnp.mean(o * o, axis=-1, keepdims=True) + EPS) * ng_ref[...]
        z = z_ref[:, sl]
        ob_parts.append((o * (z * jax.nn.sigmoid(z))).astype(BF16))
    ob = jnp.concatenate(ob_parts, axis=-1)
    ya = _dot(oa_ref[...], wa_ref[...])
    yb = _dot(ob, wb_ref[...])
    merged = jax.nn.sigmoid(ma_ref[...]) * ya + jax.nn.sigmoid(mb_ref[...]) * yb
    x1_ref[...] = x_ref[...] + _dot(merged.astype(BF16), wo_ref[...])


def _merge(x2, oa, o_f, o_b, p2, norm_g, wa, wb, wo, tm=512):
    n = x2.shape[0]
    const = lambda i: (0, 0)
    return pl.pallas_call(
        _merge_kernel,
        grid=(n // tm,),
        in_specs=[pl.BlockSpec((tm, D), lambda i: (i, 0)),
                  pl.BlockSpec((tm, HW), lambda i: (i, 0)),
                  pl.BlockSpec((tm, HW), lambda i: (i, 0)),
                  pl.BlockSpec((tm, HW), lambda i: (i, 0)),
                  pl.BlockSpec((tm, HW), lambda i: (i, C_Z // HW)),
                  pl.BlockSpec((tm, D), lambda i: (i, C_MA // D)),
                  pl.BlockSpec((tm, D), lambda i: (i, C_MB // D)),
                  pl.BlockSpec((1, HD), const),
                  pl.BlockSpec((HW, D), const),
                  pl.BlockSpec((HW, D), const),
                  pl.BlockSpec((D, D), const)],
        out_specs=pl.BlockSpec((tm, D), lambda i: (i, 0)),
        out_shape=jax.ShapeDtypeStruct((n, D), F32),
        compiler_params=_params(("parallel",)),
        name="merge",
    )(x2, oa, o_f, o_b, p2, p2, p2, norm_g.reshape(1, HD), wa, wb, wo)


def _top_rows(s, payload=None):
    r = s.shape[0]
    row = lax.broadcasted_iota(jnp.int32, s.shape, 0).astype(F32)
    vals, idxs = [], []
    for _ in range(P_TOPK):
        m = jnp.max(s, axis=0, keepdims=True)
        first = jnp.min(jnp.where(s == m, row, float(r)), axis=0, keepdims=True)
        sel = row == first
        vals.append(m)
        idxs.append(first if payload is None else jnp.max(jnp.where(sel, payload, -1.0), axis=0, keepdims=True))
        s = jnp.where(sel, -jnp.inf, s)
    return vals, idxs


def _stack_rows(rows):
    t = rows[0].shape[1]
    row = lax.broadcasted_iota(jnp.int32, (len(rows), t), 0)
    out = jnp.zeros((len(rows), t), F32)
    for i, r in enumerate(rows):
        out = jnp.where(row == i, r, out)
    return out


def _pair_candidates(v1, i1, v2, i2):
    v1s, i1s, v2s, i2s = _stack_rows(v1), _stack_rows(i1), _stack_rows(v2), _stack_rows(i2)
    brow = lax.broadcasted_iota(jnp.int32, (8, v2s.shape[1]), 0)
    cand = [v1[0] + v2s]
    cidx = [i1[0] * P_NKEYS + i2s]
    for a in range(1, 8):
        ok = brow < P_TOPK // (a + 1)
        cand.append(jnp.where(ok, v1[a] + v2s[0:8], -jnp.inf))
        cidx.append(i1[a] * P_NKEYS + i2s[0:8])
    cand.append(v1s[8:] + v2[0])
    cidx.append(i1s[8:] * P_NKEYS + i2[0])
    return jnp.concatenate(cand, axis=0), jnp.concatenate(cidx, axis=0)


def _route_kernel(x1_ref, g_ref, wq_ref, k1_ref, k2_ref, h2_ref, eidx_ref, gate_ref):
    x = x1_ref[...]
    h2 = x * lax.rsqrt(jnp.mean(x * x, axis=-1, keepdims=True) + EPS) * g_ref[...]
    h2_ref[...] = h2
    qp = _dot(h2.astype(BF16), wq_ref[...])
    gates, eidxs = [], []
    for h in range(P_HEADS):
        q1 = qp[:, h * P_DK:h * P_DK + P_DK // 2].astype(BF16)
        q2 = qp[:, h * P_DK + P_DK // 2:(h + 1) * P_DK].astype(BF16)
        s1 = _dot_nt(k1_ref[h], q1)
        s2 = _dot_nt(k2_ref[h], q2)
        v1, i1 = _top_rows(s1)
        v2, i2 = _top_rows(s2)
        cand, cidx = _pair_candidates(v1, i1, v2, i2)
        sc, ei = _top_rows(cand, cidx)
        ex = jnp.exp(_stack_rows(sc) - sc[0])
        gates.append(ex / jnp.sum(ex, axis=0, keepdims=True))
        eidxs.append(_stack_rows(ei).astype(jnp.int32))
    gate_ref[...] = jnp.concatenate(gates, axis=0).T
    eidx_ref[...] = jnp.concatenate(eidxs, axis=0).T


def _route(x1, g_ffn, wq, k1, k2, tm=128):
    n = x1.shape[0]
    return pl.pallas_call(
        _route_kernel,
        grid=(n // tm,),
        in_specs=[pl.BlockSpec((tm, D), lambda i: (i, 0)),
                  pl.BlockSpec((1, D), lambda i: (0, 0)),
                  pl.BlockSpec((D, P_HEADS * P_DK), lambda i: (0, 0)),
                  pl.BlockSpec((P_HEADS, P_NKEYS, P_DK // 2), lambda i: (0, 0, 0)),
                  pl.BlockSpec((P_HEADS, P_NKEYS, P_DK // 2), lambda i: (0, 0, 0))],
        out_specs=[pl.BlockSpec((tm, D), lambda i: (i, 0)),
                   pl.BlockSpec((tm, P_NSEL), lambda i: (i, 0)),
                   pl.BlockSpec((tm, P_NSEL), lambda i: (i, 0))],
        out_shape=[jax.ShapeDtypeStruct((n, D), F32),
                   jax.ShapeDtypeStruct((n, P_NSEL), jnp.int32),
                   jax.ShapeDtypeStruct((n, P_NSEL), F32)],
        compiler_params=_params(("parallel",)),
        name="route",
    )(x1, g_ffn.reshape(1, D), wq, k1, k2)


def _pack_table(t):
    bits = lax.bitcast_convert_type(t.astype(BF16), jnp.uint16).astype(jnp.uint32)
    packed = bits[:, :D // 2] | (bits[:, D // 2:] << 16)
    return lax.bitcast_convert_type(packed, jnp.int32).reshape(t.shape[0], SUB, LANES)


NROWS = P_NSEL * SUB
NCOL = 2 * NROWS


def _gather_rows(eidx_ref, tab_ref, g_ref, t):
    for k in range(P_NSEL):
        g_ref[SUB * k:SUB * (k + 1), :] = tab_ref[eidx_ref[t, k]]


def _two_buffer_loop(tm, gather, compute, ga_ref, gb_ref):
    gather(0, ga_ref)

    def body(i, carry):
        t = 2 * i
        gather(t + 1, gb_ref)
        compute(t, ga_ref)
        gather(jnp.minimum(t + 2, tm - 1), ga_ref)
        compute(t + 1, gb_ref)
        return carry

    lax.fori_loop(0, tm // 2, body, 0)


def _diag_mask():
    j = lax.broadcasted_iota(jnp.int32, (2 * SUB, NCOL), 0)
    c = lax.broadcasted_iota(jnp.int32, (2 * SUB, NCOL), 1)
    return (c & 7) == ((j & 3) * 2 + (j >> 2))


def _split_bf16(a):
    hi = a.astype(BF16)
    lo = (a - hi.astype(F32)).astype(BF16)
    return jnp.concatenate([hi, lo], axis=0)


def _peer_u_kernel(eidx_ref, x_ref, gate_ref, tab_ref, sel_ref, w_ref, ga_ref, gb_ref, r_ref, *, tm):
    mask = _diag_mask()

    def gather(t, g_ref):
        _gather_rows(eidx_ref, tab_ref, g_ref, t)

    def compute(t, g_ref):
        g16 = pltpu.bitcast(g_ref[...], BF16)
        o = _dot_nt(_split_bf16(x_ref[t]), g16)
        o8 = o[0:2 * SUB] + o[2 * SUB:]
        r_ref[pl.ds(t, 1), :] = jnp.sum(jnp.where(mask, o8, 0.0), axis=0, keepdims=True)

    _two_buffer_loop(tm, gather, compute, ga_ref, gb_ref)
    pre = _dot(r_ref[...], sel_ref[...], precision=HIGHEST)
    act = 0.5 * pre * (1.0 + lax.erf(pre * (2.0 ** -0.5)))
    w_ref[...] = gate_ref[...] * act


def _group_matrix():
    c = jnp.arange(NCOL, dtype=jnp.int32)[:, None] // 8
    k = jnp.arange(P_NSEL, dtype=jnp.int32)[None, :]
    return (c == k).astype(F32)


def _peer_u(eidx, h2r, gate, tab, tm=256):
    n = eidx.shape[0]
    return pl.pallas_call(
        functools.partial(_peer_u_kernel, tm=tm),
        grid=(n // tm,),
        in_specs=[pl.BlockSpec((tm, P_NSEL), lambda i: (i, 0), memory_space=pltpu.SMEM),
                  pl.BlockSpec((tm, 2 * SUB, LANES), lambda i: (i, 0, 0)),
                  pl.BlockSpec((tm, P_NSEL), lambda i: (i, 0)),
                  pl.BlockSpec(memory_space=pltpu.VMEM),
                  pl.BlockSpec((NCOL, P_NSEL), lambda i: (0, 0))],
        out_specs=pl.BlockSpec((tm, P_NSEL), lambda i: (i, 0)),
        out_shape=jax.ShapeDtypeStruct((n, P_NSEL), F32),
        scratch_shapes=[pltpu.VMEM((NROWS, LANES), jnp.int32), pltpu.VMEM((NROWS, LANES), jnp.int32),
                        pltpu.VMEM((tm, NCOL), F32)],
        compiler_params=_params(("arbitrary",)),
        name="peer_u",
    )(eidx, h2r, gate, tab, _group_matrix())


def _peer_v_kernel(eidx_ref, w_ref, tab_ref, rep_ref, out_ref, ga_ref, gb_ref, w8_ref, *, tm):
    mask = _diag_mask()
    w8_ref[...] = _dot(w_ref[...], rep_ref[...], precision=HIGHEST)

    def gather(t, g_ref):
        _gather_rows(eidx_ref, tab_ref, g_ref, t)

    def compute(t, g_ref):
        g16 = pltpu.bitcast(g_ref[...], BF16)
        wexp = jnp.where(mask, jnp.broadcast_to(w8_ref[pl.ds(t, 1), :], (2 * SUB, NCOL)), 0.0)
        o = _dot(_split_bf16(wexp), g16)
        out_ref[t] = o[0:2 * SUB] + o[2 * SUB:]

    _two_buffer_loop(tm, gather, compute, ga_ref, gb_ref)


def _peer_v(eidx, w, tab, tm=256):
    n = eidx.shape[0]
    return pl.pallas_call(
        functools.partial(_peer_v_kernel, tm=tm),
        grid=(n // tm,),
        in_specs=[pl.BlockSpec((tm, P_NSEL), lambda i: (i, 0), memory_space=pltpu.SMEM),
                  pl.BlockSpec((tm, P_NSEL), lambda i: (i, 0)),
                  pl.BlockSpec(memory_space=pltpu.VMEM),
                  pl.BlockSpec((P_NSEL, NCOL), lambda i: (0, 0))],
        out_specs=pl.BlockSpec((tm, 2 * SUB, LANES), lambda i: (i, 0, 0)),
        out_shape=jax.ShapeDtypeStruct((n, 2 * SUB, LANES), F32),
        scratch_shapes=[pltpu.VMEM((NROWS, LANES), jnp.int32), pltpu.VMEM((NROWS, LANES), jnp.int32),
                        pltpu.VMEM((tm, NCOL), F32)],
        compiler_params=_params(("arbitrary",)),
        name="peer_v",
    )(eidx, w, tab, _group_matrix().T)


def _final_kernel(x1_ref, po_ref, g_ref, y_ref):
    x = x1_ref[...] + po_ref[...]
    y_ref[...] = x * lax.rsqrt(jnp.mean(x * x, axis=-1, keepdims=True) + EPS) * g_ref[...]


def _final(x1, po, g, tm=512):
    n = x1.shape[0]
    return pl.pallas_call(
        _final_kernel,
        grid=(n // tm,),
        in_specs=[pl.BlockSpec((tm, D), lambda i: (i, 0)),
                  pl.BlockSpec((tm, D), lambda i: (i, 0)),
                  pl.BlockSpec((1, D), lambda i: (0, 0))],
        out_specs=pl.BlockSpec((tm, D), lambda i: (i, 0)),
        out_shape=jax.ShapeDtypeStruct((n, D), F32),
        compiler_params=_params(("parallel",)),
        name="final",
    )(x1, po, g.reshape(1, D))


def _pack_w_in(w_in):
    w = jnp.concatenate([w_in[:, 2048:3584], w_in[:, :2048], w_in[:, 3584:4096], w_in[:, 4112:6160],
                         w_in[:, 4096:4112], jnp.zeros((D, PW - C_SMALL - 16), w_in.dtype)], axis=1)
    return w.astype(BF16)


def _encoder(x, wts):
    b, s, _ = x.shape
    n = b * s
    x2 = x.reshape(n, D)
    p2 = _in_proj(x2, wts["g_mix"], wts["w_cat"])
    p3 = p2.reshape(b, s, PW)
    oa = _retention(p3, wts["ret_gn"]).reshape(n, HW)
    qkv, sm = _gdn_prep(p3, wts["dn_conv"], wts["dn_a_log"], wts["dn_dt_bias"])
    o_f, o_b = _gdn_scan(_gdn_terms(qkv, sm), nb=4 if b % 4 == 0 else 1)
    x1 = _merge(x2, oa, o_f.reshape(n, HW), o_b.reshape(n, HW), p2, wts["dn_norm"],
                wts["wa"], wts["wb"], wts["wo"])
    h2, eidx, gate = _route(x1, wts["g_ffn"], wts["wq"], wts["k1"], wts["k2"])
    w = _peer_u(eidx, h2.reshape(n, 2 * SUB, LANES), gate, wts["tab_u"])
    po = _peer_v(eidx, w, wts["tab_v"]).reshape(n, D)
    return _final(x1, po, wts["g_final"]).reshape(b, s, D)


def kernel(x_prompt, x_sample, g_mix, w_in, ret_gn, dn_conv, dn_a_log, dn_dt_bias, dn_norm, w_branch_a, w_branch_b, w_out, g_ffn, peer_wq, peer_k1, peer_k2, peer_u, peer_v, g_final):
    wts = dict(
        g_mix=g_mix[0], w_cat=_pack_w_in(w_in[0]), ret_gn=ret_gn[0], dn_conv=dn_conv[0],
        dn_a_log=dn_a_log[0], dn_dt_bias=dn_dt_bias[0], dn_norm=dn_norm[0],
        wa=w_branch_a[0].astype(BF16), wb=w_branch_b[0].astype(BF16), wo=w_out[0].astype(BF16),
        g_ffn=g_ffn[0], wq=peer_wq[0].astype(BF16), k1=peer_k1[0].astype(BF16), k2=peer_k2[0].astype(BF16),
        tab_u=_pack_table(peer_u[0]), tab_v=_pack_table(peer_v[0]), g_final=g_final)
    return (_encoder(x_prompt, wts), _encoder(x_sample, wts))
```

```python
import functools

import jax
import jax.numpy as jnp
import numpy as np
from jax import lax
from jax.experimental import pallas as pl
from jax.experimental.pallas import tpu as pltpu

F32 = jnp.float32
BF16 = jnp.bfloat16
HIGHEST = lax.Precision.HIGHEST

D = 1024
EPS = 1e-6
NH = 4
HD = 128
HW = NH * HD
ROPE_BASE = 10000.0
DN_CONV = 5
DN_C = 64
RET_C = 512
P_HEADS = 8
P_DK = 256
P_NKEYS = 128
P_TOPK = 16
P_NSEL = P_HEADS * P_TOPK
LANES = 128
SUB = 4
VMEM_LIMIT = 56 * 1024 * 1024

C_DNQKV = 0
C_RET = 1536
C_Z = 3584
C_MA = 4096
C_MB = 5120
C_SMALL = 6144
PW = 6272
PCH = 896


def _dot(a, b, **kw):
    return jnp.dot(a, b, preferred_element_type=F32, **kw)


def _dot_nt(a, b, **kw):
    return lax.dot_general(a, b, (((1,), (1,)), ((), ())), preferred_element_type=F32, **kw)


def _params(sem=None):
    return pltpu.CompilerParams(dimension_semantics=sem, vmem_limit_bytes=VMEM_LIMIT)


def _in_proj_kernel(x_ref, g_ref, w_ref, p_ref):
    x = x_ref[...]
    h = x * lax.rsqrt(jnp.mean(x * x, axis=-1, keepdims=True) + EPS) * g_ref[...]
    h = h.astype(BF16)
    for c in range(0, PW, PCH):
        p_ref[:, c:c + PCH] = _dot(h, w_ref[:, c:c + PCH])


def _in_proj(x2, g, w_cat, tm=256):
    n = x2.shape[0]
    return pl.pallas_call(
        _in_proj_kernel,
        grid=(n // tm,),
        in_specs=[pl.BlockSpec((tm, D), lambda i: (i, 0)),
                  pl.BlockSpec((1, D), lambda i: (0, 0)),
                  pl.BlockSpec((D, PW), lambda i: (0, 0))],
        out_specs=pl.BlockSpec((tm, PW), lambda i: (i, 0)),
        out_shape=jax.ShapeDtypeStruct((n, PW), F32),
        compiler_params=_params(("parallel",)),
        name="in_proj",
    )(x2, g.reshape(1, D), w_cat)


def _ret_kernel(q_ref, k_ref, v_ref, gt_ref, cos_ref, sin_ref, dmat_ref, vec_ref, cdec_ref, gn_ref,
                o_ref, run_ref, stash_ref):
    ph = pl.program_id(1)
    c = pl.program_id(2)
    nc = pl.num_programs(2)

    @pl.when(c == 0)
    def _():
        run_ref[...] = jnp.zeros_like(run_ref)

    cos2 = cos_ref[...]
    sinm = sin_ref[...]

    def rot(t):
        return t * cos2 + pltpu.roll(t, HD // 2, axis=1) * sinm

    def state_update(h, kh, vh, zeta, dec_row):
        kz = (kh * zeta).T.astype(BF16)
        contrib = _dot(kz, vh.astype(BF16))
        run_ref[h] = run_ref[h] * cdec_ref[dec_row:dec_row + 1, :] + contrib

    @pl.when(ph == 0)
    def _():
        ci = nc - 1 - c
        for h in range(NH):
            sl = slice(h * HD, (h + 1) * HD)
            kh = rot(k_ref[0, :, sl])
            stash_ref[ci, h] = run_ref[h]
            state_update(h, kh, v_ref[0, :, sl], vec_ref[3, h], NH + h)

    @pl.when(ph == 1)
    def _():
        for h in range(NH):
            sl = slice(h * HD, (h + 1) * HD)
            qh = rot(q_ref[0, :, sl]) * (HD ** -0.5)
            kh = rot(k_ref[0, :, sl])
            vh = v_ref[0, :, sl]
            vb = vh.astype(BF16)
            scores = _dot_nt(qh.astype(BF16), kh.astype(BF16)) * dmat_ref[h]
            o = _dot(scores.astype(BF16), vb)
            o += _dot((qh * vec_ref[0, h]).astype(BF16), run_ref[h].astype(BF16))
            o += _dot((qh * vec_ref[1, h]).astype(BF16), stash_ref[c, h].astype(BF16))
            state_update(h, kh, vh, vec_ref[2, h], h)
            mu = jnp.mean(o, axis=-1, keepdims=True)
            oc = o - mu
            var = jnp.mean(oc * oc, axis=-1, keepdims=True)
            on = oc * lax.rsqrt(var + EPS) * gn_ref[:, sl]
            gate = gt_ref[0, :, sl]
            o_ref[0, :, sl] = (on * (gate * jax.nn.sigmoid(gate))).astype(o_ref.dtype)


def _ret_tables(s):
    half = HD // 2
    inv_freq = 1.0 / (ROPE_BASE ** jnp.linspace(0.0, 1.0, half, dtype=F32))
    ang = jnp.arange(s, dtype=F32)[:, None] * inv_freq[None, :]
    cos, sin = jnp.cos(ang), jnp.sin(ang)
    cos2 = jnp.concatenate([cos, cos], axis=-1)
    sinm = jnp.concatenate([-sin, sin], axis=-1)
    lg = jnp.log(1.0 - 2.0 ** (-5.0 - jnp.arange(NH, dtype=F32)))
    lgb = lg[::-1]
    c = RET_C
    pos = jnp.arange(c, dtype=F32)
    diff = pos[:, None] - pos[None, :]
    dmat = jnp.where(diff >= 0, jnp.exp(lg[:, None, None] * jnp.maximum(diff, 0.0)),
                     jnp.exp(lgb[:, None, None] * jnp.maximum(-diff, 0.0)))
    xi_f = jnp.exp(lg[:, None] * (pos + 1.0))
    xi_b = jnp.exp(lgb[:, None] * (c - pos))
    zeta_f = jnp.exp(lg[:, None] * (c - 1 - pos))
    zeta_b = jnp.exp(lgb[:, None] * pos)
    vec = jnp.broadcast_to(jnp.stack([xi_f, xi_b, zeta_f, zeta_b])[..., None], (4, NH, c, LANES))
    cdec = jnp.concatenate([jnp.exp(lg * c), jnp.exp(lgb * c)])
    cdec = jnp.broadcast_to(cdec[:, None], (2 * NH, LANES))
    return cos2, sinm, dmat, vec, cdec


def _retention(p3, ret_gn):
    b, s, _ = p3.shape
    c = RET_C
    nc = s // c
    cos2, sinm, dmat, vec, cdec = _ret_tables(s)

    def chunk(ph, ci):
        return jnp.where(ph == 0, nc - 1 - ci, ci)

    def pspec(col):
        return pl.BlockSpec((1, c, HW), lambda bi, ph, ci: (bi, chunk(ph, ci), col))

    base = C_RET // HW
    return pl.pallas_call(
        _ret_kernel,
        grid=(b, 2, nc),
        in_specs=[pspec(base), pspec(base + 1), pspec(base + 2), pspec(base + 3),
                  pl.BlockSpec((c, HD), lambda bi, ph, ci: (chunk(ph, ci), 0)),
                  pl.BlockSpec((c, HD), lambda bi, ph, ci: (chunk(ph, ci), 0)),
                  pl.BlockSpec((NH, c, c), lambda bi, ph, ci: (0, 0, 0)),
                  pl.BlockSpec((4, NH, c, LANES), lambda bi, ph, ci: (0, 0, 0, 0)),
                  pl.BlockSpec((2 * NH, LANES), lambda bi, ph, ci: (0, 0)),
                  pl.BlockSpec((1, HW), lambda bi, ph, ci: (0, 0))],
        out_specs=pl.BlockSpec((1, c, HW), lambda bi, ph, ci: (bi, ci * ph, 0)),
        out_shape=jax.ShapeDtypeStruct((b, s, HW), BF16),
        scratch_shapes=[pltpu.VMEM((NH, HD, HD), F32), pltpu.VMEM((nc, NH, HD, HD), F32)],
        compiler_params=_params(("parallel", "arbitrary", "arbitrary")),
        name="retention",
    )(p3, p3, p3, p3, cos2, sinm, dmat, vec, cdec, ret_gn.reshape(1, HW))


def _gdn_prep_kernel(prev_ref, cur_ref, next_ref, sm_ref, cw_ref, alog_ref, dtb_ref, qkv_ref, smo_ref, *, tc):
    i = pl.program_id(1)
    last = pl.num_programs(1) - 1
    prev = jnp.where(i == 0, 0.0, prev_ref[0])
    nxt = jnp.where(i == last, 0.0, next_ref[0])
    ext = jnp.concatenate([prev, cur_ref[0], nxt], axis=0)
    nrow = tc + 16
    half = DN_CONV // 2
    y = jnp.zeros((tc, 3 * HW), F32)
    for j in range(DN_CONV):
        shifted = ext if j == half else pltpu.roll(ext, (half - j) % nrow, axis=0)
        y = y + shifted[8:8 + tc] * cw_ref[j:j + 1, :]
    y = y * jax.nn.sigmoid(y)
    for h in range(2 * NH):
        sl = slice(h * HD, (h + 1) * HD)
        t = y[:, sl]
        t = t * lax.rsqrt(jnp.sum(t * t, axis=-1, keepdims=True) + EPS)
        if h < NH:
            t = t * (HD ** -0.5)
        qkv_ref[0, :, sl] = t
    qkv_ref[0, :, 2 * HW:] = y[:, 2 * HW:]
    a = sm_ref[0]
    lane = lax.broadcasted_iota(jnp.int32, a.shape, 1)
    z = a + dtb_ref[...]
    softplus = jnp.maximum(z, 0.0) + jnp.log1p(jnp.exp(-jnp.abs(z)))
    g = -jnp.exp(alog_ref[...]) * softplus
    smo_ref[0] = jnp.where(lane < 2 * NH, jax.nn.sigmoid(a), g)


def _gdn_prep(p3, conv_w, a_log, dt_bias, tc=512):
    b, s, _ = p3.shape
    nt = s // tc
    r8 = tc // 8
    w3 = 3 * HW
    zeros8 = jnp.zeros((2 * NH,), F32)
    pad = jnp.zeros((LANES - 4 * NH,), F32)
    alog = jnp.concatenate([zeros8, a_log[0], a_log[1], pad]).reshape(1, LANES)
    dtb = jnp.concatenate([zeros8, dt_bias[0], dt_bias[1], pad]).reshape(1, LANES)
    return pl.pallas_call(
        functools.partial(_gdn_prep_kernel, tc=tc),
        grid=(b, nt),
        in_specs=[pl.BlockSpec((1, 8, w3), lambda bi, i: (bi, jnp.maximum(i * r8 - 1, 0), 0)),
                  pl.BlockSpec((1, tc, w3), lambda bi, i: (bi, i, 0)),
                  pl.BlockSpec((1, 8, w3), lambda bi, i: (bi, jnp.minimum((i + 1) * r8, s // 8 - 1), 0)),
                  pl.BlockSpec((1, tc, LANES), lambda bi, i: (bi, i, C_SMALL // LANES)),
                  pl.BlockSpec((DN_CONV, w3), lambda bi, i: (0, 0)),
                  pl.BlockSpec((1, LANES), lambda bi, i: (0, 0)),
                  pl.BlockSpec((1, LANES), lambda bi, i: (0, 0))],
        out_specs=[pl.BlockSpec((1, tc, w3), lambda bi, i: (bi, i, 0)),
                   pl.BlockSpec((1, tc, LANES), lambda bi, i: (bi, i, 0))],
        out_shape=[jax.ShapeDtypeStruct((b, s, w3), F32), jax.ShapeDtypeStruct((b, s, LANES), F32)],
        compiler_params=_params(("parallel", "parallel")),
        name="gdn_prep",
    )(p3, p3, p3, p3, conv_w, alog, dtb)


def _gdn_terms_kernel(qkv_ref, sm_ref, *out_refs, nchunk):
    c = DN_C
    row = lax.broadcasted_iota(jnp.int32, (c, c), 0)
    col = lax.broadcasted_iota(jnp.int32, (c, c), 1)
    lane = lax.broadcasted_iota(jnp.int32, (c, LANES), 1)
    gt_ref = out_refs[-1]
    ids, qn, kn, vv, gcum, beta, lmat, strict = [], [], [], [], [], [], [], []
    for ci in range(nchunk):
        rows = slice(ci * c, (ci + 1) * c)
        sm = sm_ref[0, rows, :]
        for d in range(2):
            tri = jnp.where((row <= col) if d else (row >= col), 1.0, 0.0)
            gc_all = _dot(tri, sm, precision=HIGHEST)
            gc_t = gc_all.T
            incl = (row <= col) if d else (row >= col)
            for h in range(NH):
                lg = 2 * NH + NH * d + h
                ids.append((ci, d, h))
                qn.append(qkv_ref[0, rows, h * HD:(h + 1) * HD])
                kn.append(qkv_ref[0, rows, HW + h * HD:HW + (h + 1) * HD])
                vv.append(qkv_ref[0, rows, 2 * HW + h * HD:2 * HW + (h + 1) * HD])
                beta.append(jnp.sum(jnp.where(lane == NH * d + h, sm, 0.0), axis=-1, keepdims=True))
                g = jnp.sum(jnp.where(lane == lg, gc_all, 0.0), axis=-1, keepdims=True)
                gcum.append(g)
                lmat.append(jnp.where(incl, jnp.exp(jnp.where(incl, g - gc_t[lg:lg + 1, :], 0.0)), 0.0))
                strict.append((row < col) if d else (row > col))
    n = len(ids)
    rng = range(n)
    kb = [kn[i] * beta[i] for i in rng]
    knb = [kn[i].astype(BF16) for i in rng]
    kk = [_dot_nt(kb[i].astype(BF16), knb[i]) for i in rng]
    a = [jnp.where(strict[i], kk[i] * lmat[i], 0.0) for i in rng]
    ab = [a[i].astype(BF16) for i in rng]
    p = [_dot(ab[i], ab[i]) for i in rng]
    m = [-a[i] for i in rng]
    steps = int(np.log2(c)) - 1
    for it in range(steps):
        pb = [p[i].astype(BF16) for i in rng]
        mp = [_dot(m[i].astype(BF16), pb[i]) for i in rng]
        m = [m[i] + p[i] + mp[i] for i in rng]
        if it + 1 < steps:
            p = [_dot(pb[i], pb[i]) for i in rng]
    eg = [jnp.exp(gcum[i]) for i in rng]
    rhs = [jnp.concatenate([vv[i] * beta[i], kb[i] * eg[i]], axis=-1) for i in rng]
    corr = [_dot(m[i].astype(BF16), rhs[i].astype(BF16)) for i in rng]
    attn = [_dot_nt(qn[i].astype(BF16), knb[i]) for i in rng]
    for i, (ci, d, h) in enumerate(ids):
        u_ref, w_ref, qd_ref, kdt_ref, att_ref = out_refs[5 * d:5 * d + 5]
        rows = slice(ci * c, (ci + 1) * c)
        sl = slice(h * HD, (h + 1) * HD)
        sol = rhs[i] + corr[i]
        gl = gcum[i][0:1, :] if d else gcum[i][c - 1:c, :]
        u_ref[0, rows, sl] = sol[:, :HD]
        w_ref[0, rows, sl] = sol[:, HD:].astype(BF16)
        qd_ref[0, rows, sl] = (qn[i] * eg[i]).astype(BF16)
        kdt_ref[0, ci, h] = (kn[i] * jnp.exp(gl - gcum[i])).T.astype(BF16)
        att_ref[0, ci, h] = (attn[i] * lmat[i]).astype(BF16)
        gt_ref[0, ci, NH * d + h:NH * d + h + 1, :] = jnp.broadcast_to(jnp.exp(gl), (1, LANES))


def _gdn_terms(qkv, sm, nchunk=2):
    b, s, w3 = qkv.shape
    c = DN_C
    nc = s // c
    r = nchunk * c
    tok = lambda dt: (pl.BlockSpec((1, r, HW), lambda bi, i: (bi, i, 0)), jax.ShapeDtypeStruct((b, s, HW), dt))
    kdt = (pl.BlockSpec((1, nchunk, NH, HD, c), lambda bi, i: (bi, i, 0, 0, 0)),
           jax.ShapeDtypeStruct((b, nc, NH, HD, c), BF16))
    att = (pl.BlockSpec((1, nchunk, NH, c, c), lambda bi, i: (bi, i, 0, 0, 0)),
           jax.ShapeDtypeStruct((b, nc, NH, c, c), BF16))
    gt = (pl.BlockSpec((1, nchunk, 2 * NH, LANES), lambda bi, i: (bi, i, 0, 0)),
          jax.ShapeDtypeStruct((b, nc, 2 * NH, LANES), F32))
    outs = [tok(F32), tok(BF16), tok(BF16), kdt, att] * 2 + [gt]
    return pl.pallas_call(
        functools.partial(_gdn_terms_kernel, nchunk=nchunk),
        grid=(b, nc // nchunk),
        in_specs=[pl.BlockSpec((1, r, w3), lambda bi, i: (bi, i, 0)),
                  pl.BlockSpec((1, r, LANES), lambda bi, i: (bi, i, 0))],
        out_specs=[o[0] for o in outs],
        out_shape=[o[1] for o in outs],
        compiler_params=_params(("parallel", "parallel")),
        name="gdn_terms",
    )(qkv, sm)


def _gdn_scan_kernel(*refs, nb):
    (uf, wf, qdf, kdtf, attf, gtf, ub, wb, qdb, kdtb, attb, gtb, of_ref, ob_ref, s_ref) = refs

    @pl.when(pl.program_id(1) == 0)
    def _():
        s_ref[...] = jnp.zeros_like(s_ref)

    dirs = ((uf, wf, qdf, kdtf, attf, gtf, of_ref), (ub, wb, qdb, kdtb, attb, gtb, ob_ref))
    ids = [(bi, d, h) for bi in range(nb) for d in range(2) for h in range(NH)]
    sl = [slice(h * HD, (h + 1) * HD) for h in range(NH)]
    s = [s_ref[bi, d, h] for bi, d, h in ids]
    sb = [t.astype(BF16) for t in s]
    ws = [_dot(dirs[d][1][bi, :, sl[h]], sb[i]) for i, (bi, d, h) in enumerate(ids)]
    qs = [_dot(dirs[d][2][bi, :, sl[h]], sb[i]) for i, (bi, d, h) in enumerate(ids)]
    vnb = [(dirs[d][0][bi, :, sl[h]] - ws[i]).astype(BF16) for i, (bi, d, h) in enumerate(ids)]
    av = [_dot(dirs[d][4][bi, 0, h], vnb[i]) for i, (bi, d, h) in enumerate(ids)]
    kv = [_dot(dirs[d][3][bi, 0, h], vnb[i]) for i, (bi, d, h) in enumerate(ids)]
    for i, (bi, d, h) in enumerate(ids):
        dirs[d][6][bi, :, sl[h]] = qs[i] + av[i]
        gt = dirs[d][5][bi, 0, NH * d + h:NH * d + h + 1, :]
        s_ref[bi, d, h] = s[i] * gt + kv[i]


def _gdn_scan(terms, nb):
    uf, wf, qdf, kdtf, attf, ub, wb, qdb, kdtb, attb, gt = terms
    b, s, _ = uf.shape
    c = DN_C
    nc = s // c
    fwd = lambda bi, i: i
    bwd = lambda bi, i: nc - 1 - i

    def specs(pos):
        return [pl.BlockSpec((nb, c, HW), lambda bi, i: (bi, pos(bi, i), 0)),
                pl.BlockSpec((nb, c, HW), lambda bi, i: (bi, pos(bi, i), 0)),
                pl.BlockSpec((nb, c, HW), lambda bi, i: (bi, pos(bi, i), 0)),
                pl.BlockSpec((nb, 1, NH, HD, c), lambda bi, i: (bi, pos(bi, i), 0, 0, 0)),
                pl.BlockSpec((nb, 1, NH, c, c), lambda bi, i: (bi, pos(bi, i), 0, 0, 0)),
                pl.BlockSpec((nb, 1, 2 * NH, LANES), lambda bi, i: (bi, pos(bi, i), 0, 0))]

    return pl.pallas_call(
        functools.partial(_gdn_scan_kernel, nb=nb),
        grid=(b // nb, nc),
        in_specs=specs(fwd) + specs(bwd),
        out_specs=[pl.BlockSpec((nb, c, HW), lambda bi, i: (bi, i, 0)),
                   pl.BlockSpec((nb, c, HW), lambda bi, i: (bi, nc - 1 - i, 0))],
        out_shape=[jax.ShapeDtypeStruct((b, s, HW), F32), jax.ShapeDtypeStruct((b, s, HW), F32)],
        scratch_shapes=[pltpu.VMEM((nb, 2, NH, HD, HD), F32)],
        compiler_params=_params(("parallel", "arbitrary")),
        name="gdn_scan",
    )(uf, wf, qdf, kdtf, attf, gt, ub, wb, qdb, kdtb, attb, gt)


def _merge_kernel(x_ref, oa_ref, of_ref, ob_ref, z_ref, ma_ref, mb_ref, ng_ref, wa_ref, wb_ref, wo_ref, x1_ref):
    ob_parts = []
    for h in range(NH):
        sl = slice(h * HD, (h + 1) * HD)
        o = of_ref[:, sl] + ob_ref[:, sl]
        o = o * lax.rsqrt(jnp.mean(o * o, axis=-1, keepdims=True) + EPS) * ng_ref[...]
        z = z_ref[:, sl]
        ob_parts.append((o * (z * jax.nn.sigmoid(z))).astype(BF16))
    ob = jnp.concatenate(ob_parts, axis=-1)
    ya = _dot(oa_ref[...], wa_ref[...])
    yb = _dot(ob, wb_ref[...])
    merged = jax.nn.sigmoid(ma_ref[...]) * ya + jax.nn.sigmoid(mb_ref[...]) * yb
    x1_ref[...] = x_ref[...] + _dot(merged.astype(BF16), wo_ref[...])


def _merge(x2, oa, o_f, o_b, p2, norm_g, wa, wb, wo, tm=512):
    n = x2.shape[0]
    const = lambda i: (0, 0)
    return pl.pallas_call(
        _merge_kernel,
        grid=(n // tm,),
        in_specs=[pl.BlockSpec((tm, D), lambda i: (i, 0)),
                  pl.BlockSpec((tm, HW), lambda i: (i, 0)),
                  pl.BlockSpec((tm, HW), lambda i: (i, 0)),
                  pl.BlockSpec((tm, HW), lambda i: (i, 0)),
                  pl.BlockSpec((tm, HW), lambda i: (i, C_Z // HW)),
                  pl.BlockSpec((tm, D), lambda i: (i, C_MA // D)),
                  pl.BlockSpec((tm, D), lambda i: (i, C_MB // D)),
                  pl.BlockSpec((1, HD), const),
                  pl.BlockSpec((HW, D), const),
                  pl.BlockSpec((HW, D), const),
                  pl.BlockSpec((D, D), const)],
        out_specs=pl.BlockSpec((tm, D), lambda i: (i, 0)),
        out_shape=jax.ShapeDtypeStruct((n, D), F32),
        compiler_params=_params(("parallel",)),
        name="merge",
    )(x2, oa, o_f, o_b, p2, p2, p2, norm_g.reshape(1, HD), wa, wb, wo)


def _top_rows(s, payload=None):
    r = s.shape[0]
    row = lax.broadcasted_iota(jnp.int32, s.shape, 0).astype(F32)
    vals, idxs = [], []
    for _ in range(P_TOPK):
        m = jnp.max(s, axis=0, keepdims=True)
        first = jnp.min(jnp.where(s == m, row, float(r)), axis=0, keepdims=True)
        sel = row == first
        vals.append(m)
        idxs.append(first if payload is None else jnp.max(jnp.where(sel, payload, -1.0), axis=0, keepdims=True))
        s = jnp.where(sel, -jnp.inf, s)
    return vals, idxs


def _stack_rows(rows):
    t = rows[0].shape[1]
    row = lax.broadcasted_iota(jnp.int32, (len(rows), t), 0)
    out = jnp.zeros((len(rows), t), F32)
    for i, r in enumerate(rows):
        out = jnp.where(row == i, r, out)
    return out


def _pair_candidates(v1, i1, v2, i2):
    v1s, i1s, v2s, i2s = _stack_rows(v1), _stack_rows(i1), _stack_rows(v2), _stack_rows(i2)
    brow = lax.broadcasted_iota(jnp.int32, (8, v2s.shape[1]), 0)
    cand = [v1[0] + v2s]
    cidx = [i1[0] * P_NKEYS + i2s]
    for a in range(1, 8):
        ok = brow < P_TOPK // (a + 1)
        cand.append(jnp.where(ok, v1[a] + v2s[0:8], -jnp.inf))
        cidx.append(i1[a] * P_NKEYS + i2s[0:8])
    cand.append(v1s[8:] + v2[0])
    cidx.append(i1s[8:] * P_NKEYS + i2[0])
    return jnp.concatenate(cand, axis=0), jnp.concatenate(cidx, axis=0)


def _route_kernel(x1_ref, g_ref, wq_ref, k1_ref, k2_ref, h2_ref, eidx_ref, gate_ref):
    x = x1_ref[...]
    h2 = x * lax.rsqrt(jnp.mean(x * x, axis=-1, keepdims=True) + EPS) * g_ref[...]
    h2_ref[...] = h2
    qp = _dot(h2.astype(BF16), wq_ref[...])
    gates, eidxs = [], []
    for h in range(P_HEADS):
        q1 = qp[:, h * P_DK:h * P_DK + P_DK // 2].astype(BF16)
        q2 = qp[:, h * P_DK + P_DK // 2:(h + 1) * P_DK].astype(BF16)
        s1 = _dot_nt(k1_ref[h], q1)
        s2 = _dot_nt(k2_ref[h], q2)
        v1, i1 = _top_rows(s1)
        v2, i2 = _top_rows(s2)
        cand, cidx = _pair_candidates(v1, i1, v2, i2)
        sc, ei = _top_rows(cand, cidx)
        ex = jnp.exp(_stack_rows(sc) - sc[0])
        gates.append(ex / jnp.sum(ex, axis=0, keepdims=True))
        eidxs.append((_stack_rows(ei) * float(SUB)).astype(jnp.int32))
    gate_ref[...] = jnp.concatenate(gates, axis=0).T
    eidx_ref[...] = jnp.concatenate(eidxs, axis=0).T


def _route(x1, g_ffn, wq, k1, k2, tm=128):
    n = x1.shape[0]
    return pl.pallas_call(
        _route_kernel,
        grid=(n // tm,),
        in_specs=[pl.BlockSpec((tm, D), lambda i: (i, 0)),
                  pl.BlockSpec((1, D), lambda i: (0, 0)),
                  pl.BlockSpec((D, P_HEADS * P_DK), lambda i: (0, 0)),
                  pl.BlockSpec((P_HEADS, P_NKEYS, P_DK // 2), lambda i: (0, 0, 0)),
                  pl.BlockSpec((P_HEADS, P_NKEYS, P_DK // 2), lambda i: (0, 0, 0))],
        out_specs=[pl.BlockSpec((tm, D), lambda i: (i, 0)),
                   pl.BlockSpec((tm, P_NSEL), lambda i: (i, 0)),
                   pl.BlockSpec((tm, P_NSEL), lambda i: (i, 0))],
        out_shape=[jax.ShapeDtypeStruct((n, D), F32),
                   jax.ShapeDtypeStruct((n, P_NSEL), jnp.int32),
                   jax.ShapeDtypeStruct((n, P_NSEL), F32)],
        compiler_params=_params(("parallel",)),
        name="route",
    )(x1, g_ffn.reshape(1, D), wq, k1, k2)


def _pack_table(t):
    bits = lax.bitcast_convert_type(t.astype(BF16), jnp.uint16).astype(jnp.uint32)
    packed = bits[:, :D // 2] | (bits[:, D // 2:] << 16)
    return lax.bitcast_convert_type(packed, jnp.int32).reshape(t.shape[0] * SUB, LANES)


PEER_UNROLL = 8
NROWS = P_NSEL * SUB
NCOL = 2 * NROWS


def _gather_rows(off_ref, tab_ref, g_ref, t):
    off = 0
    for k in range(P_NSEL):
        off = pl.multiple_of(off_ref[t, k], SUB)
        g_ref[SUB * k:SUB * (k + 1), :] = tab_ref[pl.ds(off, SUB), :]
    return off


def _two_buffer_loop(tm, gather, compute, ga_ref, gb_ref):
    gather(0, ga_ref)
    bufs = (ga_ref, gb_ref)

    def body(i, carry):
        t = PEER_UNROLL * i
        last = 0
        for j in range(PEER_UNROLL):
            compute(t + j, bufs[j % 2])
            nxt = jnp.minimum(t + j + 1, tm - 1) + jnp.minimum(last, 0)
            last = gather(nxt, bufs[(j + 1) % 2])
        return carry

    lax.fori_loop(0, tm // PEER_UNROLL, body, 0)


def _diag_mask():
    j = lax.broadcasted_iota(jnp.int32, (2 * SUB, NCOL), 0)
    c = lax.broadcasted_iota(jnp.int32, (2 * SUB, NCOL), 1)
    return (c & 7) == ((j & 3) * 2 + (j >> 2))


def _split_bf16(a):
    hi = a.astype(BF16)
    lo = (a - hi.astype(F32)).astype(BF16)
    return jnp.concatenate([hi, lo], axis=0)


def _peer_u_kernel(eidx_ref, x_ref, gate_ref, tab_ref, sel_ref, w_ref, ga_ref, gb_ref, r_ref, *, tm):
    mask = _diag_mask()

    def gather(t, g_ref):
        return _gather_rows(eidx_ref, tab_ref, g_ref, t)

    def compute(t, g_ref):
        g16 = pltpu.bitcast(g_ref[...], BF16)
        o = _dot_nt(_split_bf16(x_ref[t]), g16)
        o8 = o[0:2 * SUB] + o[2 * SUB:]
        r_ref[pl.ds(t, 1), :] = jnp.sum(jnp.where(mask, o8, 0.0), axis=0, keepdims=True)

    _two_buffer_loop(tm, gather, compute, ga_ref, gb_ref)
    pre = _dot(r_ref[...], sel_ref[...], precision=HIGHEST)
    act = 0.5 * pre * (1.0 + lax.erf(pre * (2.0 ** -0.5)))
    w_ref[...] = gate_ref[...] * act


def _group_matrix():
    c = jnp.arange(NCOL, dtype=jnp.int32)[:, None] // 8
    k = jnp.arange(P_NSEL, dtype=jnp.int32)[None, :]
    return (c == k).astype(F32)


def _peer_u(eidx, h2r, gate, tab, tm=256):
    n = eidx.shape[0]
    return pl.pallas_call(
        functools.partial(_peer_u_kernel, tm=tm),
        grid=(n // tm,),
        in_specs=[pl.BlockSpec((tm, P_NSEL), lambda i: (i, 0), memory_space=pltpu.SMEM),
                  pl.BlockSpec((tm, 2 * SUB, LANES), lambda i: (i, 0, 0)),
                  pl.BlockSpec((tm, P_NSEL), lambda i: (i, 0)),
                  pl.BlockSpec(memory_space=pltpu.VMEM),
                  pl.BlockSpec((NCOL, P_NSEL), lambda i: (0, 0))],
        out_specs=pl.BlockSpec((tm, P_NSEL), lambda i: (i, 0)),
        out_shape=jax.ShapeDtypeStruct((n, P_NSEL), F32),
        scratch_shapes=[pltpu.VMEM((NROWS, LANES), jnp.int32), pltpu.VMEM((NROWS, LANES), jnp.int32),
                        pltpu.VMEM((tm, NCOL), F32)],
        compiler_params=_params(("arbitrary",)),
        name="peer_u",
    )(eidx, h2r, gate, tab, _group_matrix())


def _peer_v_kernel(eidx_ref, w_ref, tab_ref, rep_ref, out_ref, ga_ref, gb_ref, w8_ref, *, tm):
    mask = _diag_mask()
    w8_ref[...] = _dot(w_ref[...], rep_ref[...], precision=HIGHEST)

    def gather(t, g_ref):
        return _gather_rows(eidx_ref, tab_ref, g_ref, t)

    def compute(t, g_ref):
        g16 = pltpu.bitcast(g_ref[...], BF16)
        wexp = jnp.where(mask, jnp.broadcast_to(w8_ref[pl.ds(t, 1), :], (2 * SUB, NCOL)), 0.0)
        o = _dot(_split_bf16(wexp), g16)
        out_ref[t] = o[0:2 * SUB] + o[2 * SUB:]

    _two_buffer_loop(tm, gather, compute, ga_ref, gb_ref)


def _peer_v(eidx, w, tab, tm=256):
    n = eidx.shape[0]
    return pl.pallas_call(
        functools.partial(_peer_v_kernel, tm=tm),
        grid=(n // tm,),
        in_specs=[pl.BlockSpec((tm, P_NSEL), lambda i: (i, 0), memory_space=pltpu.SMEM),
                  pl.BlockSpec((tm, P_NSEL), lambda i: (i, 0)),
                  pl.BlockSpec(memory_space=pltpu.VMEM),
                  pl.BlockSpec((P_NSEL, NCOL), lambda i: (0, 0))],
        out_specs=pl.BlockSpec((tm, 2 * SUB, LANES), lambda i: (i, 0, 0)),
        out_shape=jax.ShapeDtypeStruct((n, 2 * SUB, LANES), F32),
        scratch_shapes=[pltpu.VMEM((NROWS, LANES), jnp.int32), pltpu.VMEM((NROWS, LANES), jnp.int32),
                        pltpu.VMEM((tm, NCOL), F32)],
        compiler_params=_params(("arbitrary",)),
        name="peer_v",
    )(eidx, w, tab, _group_matrix().T)


def _final_kernel(x1_ref, po_ref, g_ref, y_ref):
    x = x1_ref[...] + po_ref[...]
    y_ref[...] = x * lax.rsqrt(jnp.mean(x * x, axis=-1, keepdims=True) + EPS) * g_ref[...]


def _final(x1, po, g, tm=512):
    n = x1.shape[0]
    return pl.pallas_call(
        _final_kernel,
        grid=(n // tm,),
        in_specs=[pl.BlockSpec((tm, D), lambda i: (i, 0)),
                  pl.BlockSpec((tm, D), lambda i: (i, 0)),
                  pl.BlockSpec((1, D), lambda i: (0, 0))],
        out_specs=pl.BlockSpec((tm, D), lambda i: (i, 0)),
        out_shape=jax.ShapeDtypeStruct((n, D), F32),
        compiler_params=_params(("parallel",)),
        name="final",
    )(x1, po, g.reshape(1, D))


def _pack_w_in(w_in):
    w = jnp.concatenate([w_in[:, 2048:3584], w_in[:, :2048], w_in[:, 3584:4096], w_in[:, 4112:6160],
                         w_in[:, 4096:4112], jnp.zeros((D, PW - C_SMALL - 16), w_in.dtype)], axis=1)
    return w.astype(BF16)


def _encoder(x, wts):
    b, s, _ = x.shape
    n = b * s
    x2 = x.reshape(n, D)
    p2 = _in_proj(x2, wts["g_mix"], wts["w_cat"])
    p3 = p2.reshape(b, s, PW)
    oa = _retention(p3, wts["ret_gn"]).reshape(n, HW)
    qkv, sm = _gdn_prep(p3, wts["dn_conv"], wts["dn_a_log"], wts["dn_dt_bias"])
    o_f, o_b = _gdn_scan(_gdn_terms(qkv, sm), nb=4 if b % 4 == 0 else 1)
    x1 = _merge(x2, oa, o_f.reshape(n, HW), o_b.reshape(n, HW), p2, wts["dn_norm"],
                wts["wa"], wts["wb"], wts["wo"])
    h2, eidx, gate = _route(x1, wts["g_ffn"], wts["wq"], wts["k1"], wts["k2"])
    w = _peer_u(eidx, h2.reshape(n, 2 * SUB, LANES), gate, wts["tab_u"])
    po = _peer_v(eidx, w, wts["tab_v"]).reshape(n, D)
    return _final(x1, po, wts["g_final"]).reshape(b, s, D)


def kernel(x_prompt, x_sample, g_mix, w_in, ret_gn, dn_conv, dn_a_log, dn_dt_bias, dn_norm, w_branch_a, w_branch_b, w_out, g_ffn, peer_wq, peer_k1, peer_k2, peer_u, peer_v, g_final):
    wts = dict(
        g_mix=g_mix[0], w_cat=_pack_w_in(w_in[0]), ret_gn=ret_gn[0], dn_conv=dn_conv[0],
        dn_a_log=dn_a_log[0], dn_dt_bias=dn_dt_bias[0], dn_norm=dn_norm[0],
        wa=w_branch_a[0].astype(BF16), wb=w_branch_b[0].astype(BF16), wo=w_out[0].astype(BF16),
        g_ffn=g_ffn[0], wq=peer_wq[0].astype(BF16), k1=peer_k1[0].astype(BF16), k2=peer_k2[0].astype(BF16),
        tab_u=_pack_table(peer_u[0]), tab_v=_pack_table(peer_v[0]), g_final=g_final)
    return (_encoder(x_prompt, wts), _encoder(x_sample, wts))
```

```python
import functools

import jax
import jax.numpy as jnp
import numpy as np
from jax import lax
from jax.experimental import pallas as pl
from jax.experimental.pallas import tpu as pltpu

F32 = jnp.float32
BF16 = jnp.bfloat16
HIGHEST = lax.Precision.HIGHEST

D = 1024
EPS = 1e-6
NH = 4
HD = 128
HW = NH * HD
ROPE_BASE = 10000.0
DN_CONV = 5
DN_C = 64
RET_C = 512
P_HEADS = 8
P_DK = 256
P_NKEYS = 128
P_TOPK = 16
P_NSEL = P_HEADS * P_TOPK
LANES = 128
SUB = 4
VMEM_LIMIT = 56 * 1024 * 1024

C_DNQKV = 0
C_RET = 1536
C_Z = 3584
C_MA = 4096
C_MB = 5120
C_SMALL = 6144
PW = 6272
PCH = 896


def _dot(a, b, **kw):
    return jnp.dot(a, b, preferred_element_type=F32, **kw)


def _dot_nt(a, b, **kw):
    return lax.dot_general(a, b, (((1,), (1,)), ((), ())), preferred_element_type=F32, **kw)


def _params(sem=None):
    return pltpu.CompilerParams(dimension_semantics=sem, vmem_limit_bytes=VMEM_LIMIT)


def _in_proj_kernel(x_ref, g_ref, w_ref, p_ref):
    x = x_ref[...]
    h = x * lax.rsqrt(jnp.mean(x * x, axis=-1, keepdims=True) + EPS) * g_ref[...]
    h = h.astype(BF16)
    for c in range(0, PW, PCH):
        p_ref[:, c:c + PCH] = _dot(h, w_ref[:, c:c + PCH])


def _in_proj(x2, g, w_cat, tm=256):
    n = x2.shape[0]
    return pl.pallas_call(
        _in_proj_kernel,
        grid=(n // tm,),
        in_specs=[pl.BlockSpec((tm, D), lambda i: (i, 0)),
                  pl.BlockSpec((1, D), lambda i: (0, 0)),
                  pl.BlockSpec((D, PW), lambda i: (0, 0))],
        out_specs=pl.BlockSpec((tm, PW), lambda i: (i, 0)),
        out_shape=jax.ShapeDtypeStruct((n, PW), F32),
        compiler_params=_params(("parallel",)),
        name="in_proj",
    )(x2, g.reshape(1, D), w_cat)


def _ret_kernel(q_ref, k_ref, v_ref, gt_ref, cos_ref, sin_ref, dmat_ref, vec_ref, cdec_ref, gn_ref,
                o_ref, run_ref, stash_ref):
    ph = pl.program_id(1)
    c = pl.program_id(2)
    nc = pl.num_programs(2)

    @pl.when(c == 0)
    def _():
        run_ref[...] = jnp.zeros_like(run_ref)

    cos2 = cos_ref[...]
    sinm = sin_ref[...]

    def rot(t):
        return t * cos2 + pltpu.roll(t, HD // 2, axis=1) * sinm

    def state_update(h, kh, vh, zeta, dec_row):
        kz = (kh * zeta).T.astype(BF16)
        contrib = _dot(kz, vh.astype(BF16))
        run_ref[h] = run_ref[h] * cdec_ref[dec_row:dec_row + 1, :] + contrib

    @pl.when(ph == 0)
    def _():
        ci = nc - 1 - c
        for h in range(NH):
            sl = slice(h * HD, (h + 1) * HD)
            kh = rot(k_ref[0, :, sl])
            stash_ref[ci, h] = run_ref[h]
            state_update(h, kh, v_ref[0, :, sl], vec_ref[3, h], NH + h)

    @pl.when(ph == 1)
    def _():
        for h in range(NH):
            sl = slice(h * HD, (h + 1) * HD)
            qh = rot(q_ref[0, :, sl]) * (HD ** -0.5)
            kh = rot(k_ref[0, :, sl])
            vh = v_ref[0, :, sl]
            vb = vh.astype(BF16)
            scores = _dot_nt(qh.astype(BF16), kh.astype(BF16)) * dmat_ref[h]
            o = _dot(scores.astype(BF16), vb)
            o += _dot((qh * vec_ref[0, h]).astype(BF16), run_ref[h].astype(BF16))
            o += _dot((qh * vec_ref[1, h]).astype(BF16), stash_ref[c, h].astype(BF16))
            state_update(h, kh, vh, vec_ref[2, h], h)
            mu = jnp.mean(o, axis=-1, keepdims=True)
            oc = o - mu
            var = jnp.mean(oc * oc, axis=-1, keepdims=True)
            on = oc * lax.rsqrt(var + EPS) * gn_ref[:, sl]
            gate = gt_ref[0, :, sl]
            o_ref[0, :, sl] = (on * (gate * jax.nn.sigmoid(gate))).astype(o_ref.dtype)


def _ret_tables(s):
    half = HD // 2
    inv_freq = 1.0 / (ROPE_BASE ** jnp.linspace(0.0, 1.0, half, dtype=F32))
    ang = jnp.arange(s, dtype=F32)[:, None] * inv_freq[None, :]
    cos, sin = jnp.cos(ang), jnp.sin(ang)
    cos2 = jnp.concatenate([cos, cos], axis=-1)
    sinm = jnp.concatenate([-sin, sin], axis=-1)
    lg = jnp.log(1.0 - 2.0 ** (-5.0 - jnp.arange(NH, dtype=F32)))
    lgb = lg[::-1]
    c = RET_C
    pos = jnp.arange(c, dtype=F32)
    diff = pos[:, None] - pos[None, :]
    dmat = jnp.where(diff >= 0, jnp.exp(lg[:, None, None] * jnp.maximum(diff, 0.0)),
                     jnp.exp(lgb[:, None, None] * jnp.maximum(-diff, 0.0)))
    xi_f = jnp.exp(lg[:, None] * (pos + 1.0))
    xi_b = jnp.exp(lgb[:, None] * (c - pos))
    zeta_f = jnp.exp(lg[:, None] * (c - 1 - pos))
    zeta_b = jnp.exp(lgb[:, None] * pos)
    vec = jnp.broadcast_to(jnp.stack([xi_f, xi_b, zeta_f, zeta_b])[..., None], (4, NH, c, LANES))
    cdec = jnp.concatenate([jnp.exp(lg * c), jnp.exp(lgb * c)])
    cdec = jnp.broadcast_to(cdec[:, None], (2 * NH, LANES))
    return cos2, sinm, dmat, vec, cdec


def _retention(p3, ret_gn):
    b, s, _ = p3.shape
    c = RET_C
    nc = s // c
    cos2, sinm, dmat, vec, cdec = _ret_tables(s)

    def chunk(ph, ci):
        return jnp.where(ph == 0, nc - 1 - ci, ci)

    def pspec(col):
        return pl.BlockSpec((1, c, HW), lambda bi, ph, ci: (bi, chunk(ph, ci), col))

    base = C_RET // HW
    return pl.pallas_call(
        _ret_kernel,
        grid=(b, 2, nc),
        in_specs=[pspec(base), pspec(base + 1), pspec(base + 2), pspec(base + 3),
                  pl.BlockSpec((c, HD), lambda bi, ph, ci: (chunk(ph, ci), 0)),
                  pl.BlockSpec((c, HD), lambda bi, ph, ci: (chunk(ph, ci), 0)),
                  pl.BlockSpec((NH, c, c), lambda bi, ph, ci: (0, 0, 0)),
                  pl.BlockSpec((4, NH, c, LANES), lambda bi, ph, ci: (0, 0, 0, 0)),
                  pl.BlockSpec((2 * NH, LANES), lambda bi, ph, ci: (0, 0)),
                  pl.BlockSpec((1, HW), lambda bi, ph, ci: (0, 0))],
        out_specs=pl.BlockSpec((1, c, HW), lambda bi, ph, ci: (bi, ci * ph, 0)),
        out_shape=jax.ShapeDtypeStruct((b, s, HW), BF16),
        scratch_shapes=[pltpu.VMEM((NH, HD, HD), F32), pltpu.VMEM((nc, NH, HD, HD), F32)],
        compiler_params=_params(("parallel", "arbitrary", "arbitrary")),
        name="retention",
    )(p3, p3, p3, p3, cos2, sinm, dmat, vec, cdec, ret_gn.reshape(1, HW))


def _gdn_prep_kernel(prev_ref, cur_ref, next_ref, sm_ref, cw_ref, alog_ref, dtb_ref, qkv_ref, smo_ref, *, tc):
    i = pl.program_id(1)
    last = pl.num_programs(1) - 1
    prev = jnp.where(i == 0, 0.0, prev_ref[0])
    nxt = jnp.where(i == last, 0.0, next_ref[0])
    ext = jnp.concatenate([prev, cur_ref[0], nxt], axis=0)
    nrow = tc + 16
    half = DN_CONV // 2
    y = jnp.zeros((tc, 3 * HW), F32)
    for j in range(DN_CONV):
        shifted = ext if j == half else pltpu.roll(ext, (half - j) % nrow, axis=0)
        y = y + shifted[8:8 + tc] * cw_ref[j:j + 1, :]
    y = y * jax.nn.sigmoid(y)
    for h in range(2 * NH):
        sl = slice(h * HD, (h + 1) * HD)
        t = y[:, sl]
        t = t * lax.rsqrt(jnp.sum(t * t, axis=-1, keepdims=True) + EPS)
        if h < NH:
            t = t * (HD ** -0.5)
        qkv_ref[0, :, sl] = t
    qkv_ref[0, :, 2 * HW:] = y[:, 2 * HW:]
    a = sm_ref[0]
    lane = lax.broadcasted_iota(jnp.int32, a.shape, 1)
    z = a + dtb_ref[...]
    softplus = jnp.maximum(z, 0.0) + jnp.log1p(jnp.exp(-jnp.abs(z)))
    g = -jnp.exp(alog_ref[...]) * softplus
    smo_ref[0] = jnp.where(lane < 2 * NH, jax.nn.sigmoid(a), g)


def _gdn_prep(p3, conv_w, a_log, dt_bias, tc=512):
    b, s, _ = p3.shape
    nt = s // tc
    r8 = tc // 8
    w3 = 3 * HW
    zeros8 = jnp.zeros((2 * NH,), F32)
    pad = jnp.zeros((LANES - 4 * NH,), F32)
    alog = jnp.concatenate([zeros8, a_log[0], a_log[1], pad]).reshape(1, LANES)
    dtb = jnp.concatenate([zeros8, dt_bias[0], dt_bias[1], pad]).reshape(1, LANES)
    return pl.pallas_call(
        functools.partial(_gdn_prep_kernel, tc=tc),
        grid=(b, nt),
        in_specs=[pl.BlockSpec((1, 8, w3), lambda bi, i: (bi, jnp.maximum(i * r8 - 1, 0), 0)),
                  pl.BlockSpec((1, tc, w3), lambda bi, i: (bi, i, 0)),
                  pl.BlockSpec((1, 8, w3), lambda bi, i: (bi, jnp.minimum((i + 1) * r8, s // 8 - 1), 0)),
                  pl.BlockSpec((1, tc, LANES), lambda bi, i: (bi, i, C_SMALL // LANES)),
                  pl.BlockSpec((DN_CONV, w3), lambda bi, i: (0, 0)),
                  pl.BlockSpec((1, LANES), lambda bi, i: (0, 0)),
                  pl.BlockSpec((1, LANES), lambda bi, i: (0, 0))],
        out_specs=[pl.BlockSpec((1, tc, w3), lambda bi, i: (bi, i, 0)),
                   pl.BlockSpec((1, tc, LANES), lambda bi, i: (bi, i, 0))],
        out_shape=[jax.ShapeDtypeStruct((b, s, w3), F32), jax.ShapeDtypeStruct((b, s, LANES), F32)],
        compiler_params=_params(("parallel", "parallel")),
        name="gdn_prep",
    )(p3, p3, p3, p3, conv_w, alog, dtb)


def _gdn_terms_kernel(qkv_ref, sm_ref, *out_refs, nchunk):
    c = DN_C
    row = lax.broadcasted_iota(jnp.int32, (c, c), 0)
    col = lax.broadcasted_iota(jnp.int32, (c, c), 1)
    lane = lax.broadcasted_iota(jnp.int32, (c, LANES), 1)
    gt_ref = out_refs[-1]
    ids, qn, kn, vv, gcum, beta, lmat, strict = [], [], [], [], [], [], [], []
    for ci in range(nchunk):
        rows = slice(ci * c, (ci + 1) * c)
        sm = sm_ref[0, rows, :]
        for d in range(2):
            tri = jnp.where((row <= col) if d else (row >= col), 1.0, 0.0)
            gc_all = _dot(tri, sm, precision=HIGHEST)
            gc_t = gc_all.T
            incl = (row <= col) if d else (row >= col)
            for h in range(NH):
                lg = 2 * NH + NH * d + h
                ids.append((ci, d, h))
                qn.append(qkv_ref[0, rows, h * HD:(h + 1) * HD])
                kn.append(qkv_ref[0, rows, HW + h * HD:HW + (h + 1) * HD])
                vv.append(qkv_ref[0, rows, 2 * HW + h * HD:2 * HW + (h + 1) * HD])
                beta.append(jnp.sum(jnp.where(lane == NH * d + h, sm, 0.0), axis=-1, keepdims=True))
                g = jnp.sum(jnp.where(lane == lg, gc_all, 0.0), axis=-1, keepdims=True)
                gcum.append(g)
                lmat.append(jnp.where(incl, jnp.exp(jnp.where(incl, g - gc_t[lg:lg + 1, :], 0.0)), 0.0))
                strict.append((row < col) if d else (row > col))
    n = len(ids)
    rng = range(n)
    kb = [kn[i] * beta[i] for i in rng]
    knb = [kn[i].astype(BF16) for i in rng]
    kk = [_dot_nt(kb[i].astype(BF16), knb[i]) for i in rng]
    a = [jnp.where(strict[i], kk[i] * lmat[i], 0.0) for i in rng]
    ab = [a[i].astype(BF16) for i in rng]
    p = [_dot(ab[i], ab[i]) for i in rng]
    m = [-a[i] for i in rng]
    steps = int(np.log2(c)) - 1
    for it in range(steps):
        pb = [p[i].astype(BF16) for i in rng]
        mp = [_dot(m[i].astype(BF16), pb[i]) for i in rng]
        m = [m[i] + p[i] + mp[i] for i in rng]
        if it + 1 < steps:
            p = [_dot(pb[i], pb[i]) for i in rng]
    eg = [jnp.exp(gcum[i]) for i in rng]
    rhs = [jnp.concatenate([vv[i] * beta[i], kb[i] * eg[i]], axis=-1) for i in rng]
    corr = [_dot(m[i].astype(BF16), rhs[i].astype(BF16)) for i in rng]
    attn = [_dot_nt(qn[i].astype(BF16), knb[i]) for i in rng]
    for i, (ci, d, h) in enumerate(ids):
        u_ref, w_ref, qd_ref, kdt_ref, att_ref = out_refs[5 * d:5 * d + 5]
        rows = slice(ci * c, (ci + 1) * c)
        sl = slice(h * HD, (h + 1) * HD)
        sol = rhs[i] + corr[i]
        gl = gcum[i][0:1, :] if d else gcum[i][c - 1:c, :]
        u_ref[0, rows, sl] = sol[:, :HD]
        w_ref[0, rows, sl] = sol[:, HD:].astype(BF16)
        qd_ref[0, rows, sl] = (qn[i] * eg[i]).astype(BF16)
        kdt_ref[0, ci, h] = (kn[i] * jnp.exp(gl - gcum[i])).T.astype(BF16)
        att_ref[0, ci, h] = (attn[i] * lmat[i]).astype(BF16)
        gt_ref[0, ci, NH * d + h:NH * d + h + 1, :] = jnp.broadcast_to(jnp.exp(gl), (1, LANES))


def _gdn_terms(qkv, sm, nchunk=2):
    b, s, w3 = qkv.shape
    c = DN_C
    nc = s // c
    r = nchunk * c
    tok = lambda dt: (pl.BlockSpec((1, r, HW), lambda bi, i: (bi, i, 0)), jax.ShapeDtypeStruct((b, s, HW), dt))
    kdt = (pl.BlockSpec((1, nchunk, NH, HD, c), lambda bi, i: (bi, i, 0, 0, 0)),
           jax.ShapeDtypeStruct((b, nc, NH, HD, c), BF16))
    att = (pl.BlockSpec((1, nchunk, NH, c, c), lambda bi, i: (bi, i, 0, 0, 0)),
           jax.ShapeDtypeStruct((b, nc, NH, c, c), BF16))
    gt = (pl.BlockSpec((1, nchunk, 2 * NH, LANES), lambda bi, i: (bi, i, 0, 0)),
          jax.ShapeDtypeStruct((b, nc, 2 * NH, LANES), F32))
    outs = [tok(F32), tok(BF16), tok(BF16), kdt, att] * 2 + [gt]
    return pl.pallas_call(
        functools.partial(_gdn_terms_kernel, nchunk=nchunk),
        grid=(b, nc // nchunk),
        in_specs=[pl.BlockSpec((1, r, w3), lambda bi, i: (bi, i, 0)),
                  pl.BlockSpec((1, r, LANES), lambda bi, i: (bi, i, 0))],
        out_specs=[o[0] for o in outs],
        out_shape=[o[1] for o in outs],
        compiler_params=_params(("parallel", "parallel")),
        name="gdn_terms",
    )(qkv, sm)


def _gdn_scan_kernel(*refs, nb):
    (uf, wf, qdf, kdtf, attf, gtf, ub, wb, qdb, kdtb, attb, gtb, of_ref, ob_ref, s_ref) = refs

    @pl.when(pl.program_id(1) == 0)
    def _():
        s_ref[...] = jnp.zeros_like(s_ref)

    dirs = ((uf, wf, qdf, kdtf, attf, gtf, of_ref), (ub, wb, qdb, kdtb, attb, gtb, ob_ref))
    ids = [(bi, d, h) for bi in range(nb) for d in range(2) for h in range(NH)]
    sl = [slice(h * HD, (h + 1) * HD) for h in range(NH)]
    s = [s_ref[bi, d, h] for bi, d, h in ids]
    sb = [t.astype(BF16) for t in s]
    ws = [_dot(dirs[d][1][bi, :, sl[h]], sb[i]) for i, (bi, d, h) in enumerate(ids)]
    qs = [_dot(dirs[d][2][bi, :, sl[h]], sb[i]) for i, (bi, d, h) in enumerate(ids)]
    vnb = [(dirs[d][0][bi, :, sl[h]] - ws[i]).astype(BF16) for i, (bi, d, h) in enumerate(ids)]
    av = [_dot(dirs[d][4][bi, 0, h], vnb[i]) for i, (bi, d, h) in enumerate(ids)]
    kv = [_dot(dirs[d][3][bi, 0, h], vnb[i]) for i, (bi, d, h) in enumerate(ids)]
    for i, (bi, d, h) in enumerate(ids):
        dirs[d][6][bi, :, sl[h]] = qs[i] + av[i]
        gt = dirs[d][5][bi, 0, NH * d + h:NH * d + h + 1, :]
        s_ref[bi, d, h] = s[i] * gt + kv[i]


def _gdn_scan(terms, nb):
    uf, wf, qdf, kdtf, attf, ub, wb, qdb, kdtb, attb, gt = terms
    b, s, _ = uf.shape
    c = DN_C
    nc = s // c
    fwd = lambda bi, i: i
    bwd = lambda bi, i: nc - 1 - i

    def specs(pos):
        return [pl.BlockSpec((nb, c, HW), lambda bi, i: (bi, pos(bi, i), 0)),
                pl.BlockSpec((nb, c, HW), lambda bi, i: (bi, pos(bi, i), 0)),
                pl.BlockSpec((nb, c, HW), lambda bi, i: (bi, pos(bi, i), 0)),
                pl.BlockSpec((nb, 1, NH, HD, c), lambda bi, i: (bi, pos(bi, i), 0, 0, 0)),
                pl.BlockSpec((nb, 1, NH, c, c), lambda bi, i: (bi, pos(bi, i), 0, 0, 0)),
                pl.BlockSpec((nb, 1, 2 * NH, LANES), lambda bi, i: (bi, pos(bi, i), 0, 0))]

    return pl.pallas_call(
        functools.partial(_gdn_scan_kernel, nb=nb),
        grid=(b // nb, nc),
        in_specs=specs(fwd) + specs(bwd),
        out_specs=[pl.BlockSpec((nb, c, HW), lambda bi, i: (bi, i, 0)),
                   pl.BlockSpec((nb, c, HW), lambda bi, i: (bi, nc - 1 - i, 0))],
        out_shape=[jax.ShapeDtypeStruct((b, s, HW), F32), jax.ShapeDtypeStruct((b, s, HW), F32)],
        scratch_shapes=[pltpu.VMEM((nb, 2, NH, HD, HD), F32)],
        compiler_params=_params(("parallel", "arbitrary")),
        name="gdn_scan",
    )(uf, wf, qdf, kdtf, attf, gt, ub, wb, qdb, kdtb, attb, gt)


def _merge_kernel(x_ref, oa_ref, of_ref, ob_ref, z_ref, ma_ref, mb_ref, ng_ref, wa_ref, wb_ref, wo_ref, x1_ref):
    ob_parts = []
    for h in range(NH):
        sl = slice(h * HD, (h + 1) * HD)
        o = of_ref[:, sl] + ob_ref[:, sl]
        o = o * lax.rsqrt(jnp.mean(o * o, axis=-1, keepdims=True) + EPS) * ng_ref[...]
        z = z_ref[:, sl]
        ob_parts.append((o * (z * jax.nn.sigmoid(z))).astype(BF16))
    ob = jnp.concatenate(ob_parts, axis=-1)
    ya = _dot(oa_ref[...], wa_ref[...])
    yb = _dot(ob, wb_ref[...])
    merged = jax.nn.sigmoid(ma_ref[...]) * ya + jax.nn.sigmoid(mb_ref[...]) * yb
    x1_ref[...] = x_ref[...] + _dot(merged.astype(BF16), wo_ref[...])


def _merge(x2, oa, o_f, o_b, p2, norm_g, wa, wb, wo, tm=512):
    n = x2.shape[0]
    const = lambda i: (0, 0)
    return pl.pallas_call(
        _merge_kernel,
        grid=(n // tm,),
        in_specs=[pl.BlockSpec((tm, D), lambda i: (i, 0)),
                  pl.BlockSpec((tm, HW), lambda i: (i, 0)),
                  pl.BlockSpec((tm, HW), lambda i: (i, 0)),
                  pl.BlockSpec((tm, HW), lambda i: (i, 0)),
                  pl.BlockSpec((tm, HW), lambda i: (i, C_Z // HW)),
                  pl.BlockSpec((tm, D), lambda i: (i, C_MA // D)),
                  pl.BlockSpec((tm, D), lambda i: (i, C_MB // D)),
                  pl.BlockSpec((1, HD), const),
                  pl.BlockSpec((HW, D), const),
                  pl.BlockSpec((HW, D), const),
                  pl.BlockSpec((D, D), const)],
        out_specs=pl.BlockSpec((tm, D), lambda i: (i, 0)),
        out_shape=jax.ShapeDtypeStruct((n, D), F32),
        compiler_params=_params(("parallel",)),
        name="merge",
    )(x2, oa, o_f, o_b, p2, p2, p2, norm_g.reshape(1, HD), wa, wb, wo)


def _top_rows(s, payload=None):
    r = s.shape[0]
    row = lax.broadcasted_iota(jnp.int32, s.shape, 0).astype(F32)
    vals, idxs = [], []
    for _ in range(P_TOPK):
        m = jnp.max(s, axis=0, keepdims=True)
        first = jnp.min(jnp.where(s == m, row, float(r)), axis=0, keepdims=True)
        sel = row == first
        vals.append(m)
        idxs.append(first if payload is None else jnp.max(jnp.where(sel, payload, -1.0), axis=0, keepdims=True))
        s = jnp.where(sel, -jnp.inf, s)
    return vals, idxs


def _stack_rows(rows):
    t = rows[0].shape[1]
    row = lax.broadcasted_iota(jnp.int32, (len(rows), t), 0)
    out = jnp.zeros((len(rows), t), F32)
    for i, r in enumerate(rows):
        out = jnp.where(row == i, r, out)
    return out


def _pair_candidates(v1, i1, v2, i2):
    v1s, i1s, v2s, i2s = _stack_rows(v1), _stack_rows(i1), _stack_rows(v2), _stack_rows(i2)
    brow = lax.broadcasted_iota(jnp.int32, (8, v2s.shape[1]), 0)
    cand = [v1[0] + v2s]
    cidx = [i1[0] * P_NKEYS + i2s]
    for a in range(1, 8):
        ok = brow < P_TOPK // (a + 1)
        cand.append(jnp.where(ok, v1[a] + v2s[0:8], -jnp.inf))
        cidx.append(i1[a] * P_NKEYS + i2s[0:8])
    cand.append(v1s[8:] + v2[0])
    cidx.append(i1s[8:] * P_NKEYS + i2[0])
    return jnp.concatenate(cand, axis=0), jnp.concatenate(cidx, axis=0)


def _route_kernel(x1_ref, g_ref, wq_ref, k1_ref, k2_ref, h2_ref, eidx_ref, gate_ref):
    x = x1_ref[...]
    h2 = x * lax.rsqrt(jnp.mean(x * x, axis=-1, keepdims=True) + EPS) * g_ref[...]
    h2_ref[...] = h2
    qp = _dot(h2.astype(BF16), wq_ref[...])
    gates, eidxs = [], []
    for h in range(P_HEADS):
        q1 = qp[:, h * P_DK:h * P_DK + P_DK // 2].astype(BF16)
        q2 = qp[:, h * P_DK + P_DK // 2:(h + 1) * P_DK].astype(BF16)
        s1 = _dot_nt(k1_ref[h], q1)
        s2 = _dot_nt(k2_ref[h], q2)
        v1, i1 = _top_rows(s1)
        v2, i2 = _top_rows(s2)
        cand, cidx = _pair_candidates(v1, i1, v2, i2)
        sc, ei = _top_rows(cand, cidx)
        ex = jnp.exp(_stack_rows(sc) - sc[0])
        gates.append(ex / jnp.sum(ex, axis=0, keepdims=True))
        eidxs.append((_stack_rows(ei) * float(SUB)).astype(jnp.int32))
    gate_ref[...] = jnp.concatenate(gates, axis=0).T
    eidx_ref[...] = jnp.concatenate(eidxs, axis=0).T


def _route(x1, g_ffn, wq, k1, k2, tm=128):
    n = x1.shape[0]
    return pl.pallas_call(
        _route_kernel,
        grid=(n // tm,),
        in_specs=[pl.BlockSpec((tm, D), lambda i: (i, 0)),
                  pl.BlockSpec((1, D), lambda i: (0, 0)),
                  pl.BlockSpec((D, P_HEADS * P_DK), lambda i: (0, 0)),
                  pl.BlockSpec((P_HEADS, P_NKEYS, P_DK // 2), lambda i: (0, 0, 0)),
                  pl.BlockSpec((P_HEADS, P_NKEYS, P_DK // 2), lambda i: (0, 0, 0))],
        out_specs=[pl.BlockSpec((tm, D), lambda i: (i, 0)),
                   pl.BlockSpec((tm, P_NSEL), lambda i: (i, 0)),
                   pl.BlockSpec((tm, P_NSEL), lambda i: (i, 0))],
        out_shape=[jax.ShapeDtypeStruct((n, D), F32),
                   jax.ShapeDtypeStruct((n, P_NSEL), jnp.int32),
                   jax.ShapeDtypeStruct((n, P_NSEL), F32)],
        compiler_params=_params(("parallel",)),
        name="route",
    )(x1, g_ffn.reshape(1, D), wq, k1, k2)


def _pack_table(t):
    bits = lax.bitcast_convert_type(t.astype(BF16), jnp.uint16).astype(jnp.uint32)
    packed = bits[:, :D // 2] | (bits[:, D // 2:] << 16)
    return lax.bitcast_convert_type(packed, jnp.int32).reshape(t.shape[0] * SUB, LANES)


PEER_UNROLL = 16
NROWS = P_NSEL * SUB
NCOL = 2 * NROWS


OFF_WAYS = 8
OFF_PER = P_NSEL // OFF_WAYS


def _deal_offsets(eoff, tm):
    n = eoff.shape[0]
    dealt = eoff.reshape(n * OFF_PER, OFF_WAYS).T
    return [dealt[j] for j in range(OFF_WAYS)]


def _gather_rows(off_refs, tab_ref, g_ref, t):
    off = 0
    for i in range(OFF_PER):
        pos = t * OFF_PER + i
        for j, off_ref in enumerate(off_refs):
            k = OFF_WAYS * i + j
            off = pl.multiple_of(off_ref[pos], SUB)
            g_ref[SUB * k:SUB * (k + 1), :] = tab_ref[pl.ds(off, SUB), :]
    return off


def _two_buffer_loop(tm, gather, compute, ga_ref, gb_ref):
    gather(0, ga_ref)
    bufs = (ga_ref, gb_ref)

    def body(i, carry):
        t = PEER_UNROLL * i
        last = 0
        for j in range(PEER_UNROLL):
            compute(t + j, bufs[j % 2])
            nxt = jnp.minimum(t + j + 1, tm - 1) + jnp.minimum(last, 0)
            last = gather(nxt, bufs[(j + 1) % 2])
        return carry

    lax.fori_loop(0, tm // PEER_UNROLL, body, 0)


def _diag_mask():
    j = lax.broadcasted_iota(jnp.int32, (2 * SUB, NCOL), 0)
    c = lax.broadcasted_iota(jnp.int32, (2 * SUB, NCOL), 1)
    return (c & 7) == ((j & 3) * 2 + (j >> 2))


def _split_bf16(a):
    hi = a.astype(BF16)
    lo = (a - hi.astype(F32)).astype(BF16)
    return jnp.concatenate([hi, lo], axis=0)


def _peer_u_kernel(*refs, tm):
    off_refs = refs[:OFF_WAYS]
    x_ref, gate_ref, tab_ref, sel_ref, w_ref, ga_ref, gb_ref, r_ref = refs[OFF_WAYS:]
    mask = _diag_mask()

    def gather(t, g_ref):
        return _gather_rows(off_refs, tab_ref, g_ref, t)

    def compute(t, g_ref):
        g16 = pltpu.bitcast(g_ref[...], BF16)
        o = _dot_nt(_split_bf16(x_ref[t]), g16)
        o8 = o[0:2 * SUB] + o[2 * SUB:]
        r_ref[pl.ds(t, 1), :] = jnp.sum(jnp.where(mask, o8, 0.0), axis=0, keepdims=True)

    _two_buffer_loop(tm, gather, compute, ga_ref, gb_ref)
    pre = _dot(r_ref[...], sel_ref[...], precision=HIGHEST)
    act = 0.5 * pre * (1.0 + lax.erf(pre * (2.0 ** -0.5)))
    w_ref[...] = gate_ref[...] * act


def _group_matrix():
    c = jnp.arange(NCOL, dtype=jnp.int32)[:, None] // 8
    k = jnp.arange(P_NSEL, dtype=jnp.int32)[None, :]
    return (c == k).astype(F32)


def _off_specs(tm):
    return [pl.BlockSpec((tm * OFF_PER,), lambda i: (i,), memory_space=pltpu.SMEM) for _ in range(OFF_WAYS)]


def _peer_u(eoff, h2r, gate, tab, tm=256):
    n = eoff.shape[0]
    return pl.pallas_call(
        functools.partial(_peer_u_kernel, tm=tm),
        grid=(n // tm,),
        in_specs=_off_specs(tm) + [
                  pl.BlockSpec((tm, 2 * SUB, LANES), lambda i: (i, 0, 0)),
                  pl.BlockSpec((tm, P_NSEL), lambda i: (i, 0)),
                  pl.BlockSpec(memory_space=pltpu.VMEM),
                  pl.BlockSpec((NCOL, P_NSEL), lambda i: (0, 0))],
        out_specs=pl.BlockSpec((tm, P_NSEL), lambda i: (i, 0)),
        out_shape=jax.ShapeDtypeStruct((n, P_NSEL), F32),
        scratch_shapes=[pltpu.VMEM((NROWS, LANES), jnp.int32), pltpu.VMEM((NROWS, LANES), jnp.int32),
                        pltpu.VMEM((tm, NCOL), F32)],
        compiler_params=_params(("arbitrary",)),
        name="peer_u",
    )(*_deal_offsets(eoff, tm), h2r, gate, tab, _group_matrix())


def _peer_v_kernel(*refs, tm):
    off_refs = refs[:OFF_WAYS]
    w_ref, tab_ref, rep_ref, x1_ref, gf_ref, y_ref, ga_ref, gb_ref, w8_ref, po_ref = refs[OFF_WAYS:]
    mask = _diag_mask()
    w8_ref[...] = _dot(w_ref[...], rep_ref[...], precision=HIGHEST)
    row8 = lax.broadcasted_iota(jnp.int32, (2 * SUB, LANES), 0)

    def gather(t, g_ref):
        return _gather_rows(off_refs, tab_ref, g_ref, t)

    def compute(t, g_ref):
        g16 = pltpu.bitcast(g_ref[...], BF16)
        wexp = jnp.where(mask, jnp.broadcast_to(w8_ref[pl.ds(t, 1), :], (2 * SUB, NCOL)), 0.0)
        o = _dot(_split_bf16(wexp), g16)
        o8 = o[0:2 * SUB] + o[2 * SUB:]
        pieces = [jnp.sum(jnp.where(row8 == j, o8, 0.0), axis=0, keepdims=True) for j in range(2 * SUB)]
        po_ref[pl.ds(t, 1), :] = jnp.concatenate(pieces, axis=1)

    _two_buffer_loop(tm, gather, compute, ga_ref, gb_ref)
    x = x1_ref[...] + po_ref[...]
    y_ref[...] = x * lax.rsqrt(jnp.mean(x * x, axis=-1, keepdims=True) + EPS) * gf_ref[...]


def _peer_v(eoff, w, tab, x1, g_final, tm=256):
    n = eoff.shape[0]
    return pl.pallas_call(
        functools.partial(_peer_v_kernel, tm=tm),
        grid=(n // tm,),
        in_specs=_off_specs(tm) + [
                  pl.BlockSpec((tm, P_NSEL), lambda i: (i, 0)),
                  pl.BlockSpec(memory_space=pltpu.VMEM),
                  pl.BlockSpec((P_NSEL, NCOL), lambda i: (0, 0)),
                  pl.BlockSpec((tm, D), lambda i: (i, 0)),
                  pl.BlockSpec((1, D), lambda i: (0, 0))],
        out_specs=pl.BlockSpec((tm, D), lambda i: (i, 0)),
        out_shape=jax.ShapeDtypeStruct((n, D), F32),
        scratch_shapes=[pltpu.VMEM((NROWS, LANES), jnp.int32), pltpu.VMEM((NROWS, LANES), jnp.int32),
                        pltpu.VMEM((tm, NCOL), F32), pltpu.VMEM((tm, D), F32)],
        compiler_params=_params(("arbitrary",)),
        name="peer_v",
    )(*_deal_offsets(eoff, tm), w, tab, _group_matrix().T, x1, g_final.reshape(1, D))


def _pack_w_in(w_in):
    w = jnp.concatenate([w_in[:, 2048:3584], w_in[:, :2048], w_in[:, 3584:4096], w_in[:, 4112:6160],
                         w_in[:, 4096:4112], jnp.zeros((D, PW - C_SMALL - 16), w_in.dtype)], axis=1)
    return w.astype(BF16)


def _encoder(x, wts):
    b, s, _ = x.shape
    n = b * s
    x2 = x.reshape(n, D)
    p2 = _in_proj(x2, wts["g_mix"], wts["w_cat"])
    p3 = p2.reshape(b, s, PW)
    oa = _retention(p3, wts["ret_gn"]).reshape(n, HW)
    qkv, sm = _gdn_prep(p3, wts["dn_conv"], wts["dn_a_log"], wts["dn_dt_bias"])
    o_f, o_b = _gdn_scan(_gdn_terms(qkv, sm), nb=4 if b % 4 == 0 else 1)
    x1 = _merge(x2, oa, o_f.reshape(n, HW), o_b.reshape(n, HW), p2, wts["dn_norm"],
                wts["wa"], wts["wb"], wts["wo"])
    h2, eidx, gate = _route(x1, wts["g_ffn"], wts["wq"], wts["k1"], wts["k2"])
    w = _peer_u(eidx, h2.reshape(n, 2 * SUB, LANES), gate, wts["tab_u"])
    return _peer_v(eidx, w, wts["tab_v"], x1, wts["g_final"]).reshape(b, s, D)


def kernel(x_prompt, x_sample, g_mix, w_in, ret_gn, dn_conv, dn_a_log, dn_dt_bias, dn_norm, w_branch_a, w_branch_b, w_out, g_ffn, peer_wq, peer_k1, peer_k2, peer_u, peer_v, g_final):
    wts = dict(
        g_mix=g_mix[0], w_cat=_pack_w_in(w_in[0]), ret_gn=ret_gn[0], dn_conv=dn_conv[0],
        dn_a_log=dn_a_log[0], dn_dt_bias=dn_dt_bias[0], dn_norm=dn_norm[0],
        wa=w_branch_a[0].astype(BF16), wb=w_branch_b[0].astype(BF16), wo=w_out[0].astype(BF16),
        g_ffn=g_ffn[0], wq=peer_wq[0].astype(BF16), k1=peer_k1[0].astype(BF16), k2=peer_k2[0].astype(BF16),
        tab_u=_pack_table(peer_u[0]), tab_v=_pack_table(peer_v[0]), g_final=g_final)
    return (_encoder(x_prompt, wts), _encoder(x_sample, wts))
```

```python
import functools

import jax
import jax.numpy as jnp
import numpy as np
from jax import lax
from jax.experimental import pallas as pl
from jax.experimental.pallas import tpu as pltpu

F32 = jnp.float32
BF16 = jnp.bfloat16
HIGHEST = lax.Precision.HIGHEST

D = 1024
EPS = 1e-6
NH = 4
HD = 128
HW = NH * HD
ROPE_BASE = 10000.0
DN_CONV = 5
DN_C = 64
RET_C = 512
P_HEADS = 8
P_DK = 256
P_NKEYS = 128
P_TOPK = 16
P_NSEL = P_HEADS * P_TOPK
LANES = 128
SUB = 4
VMEM_LIMIT = 56 * 1024 * 1024

C_DNQKV = 0
C_RET = 1536
C_Z = 3584
C_MA = 4096
C_MB = 5120
C_SMALL = 6144
PW = 6272
PCH = 896


def _dot(a, b, **kw):
    return jnp.dot(a, b, preferred_element_type=F32, **kw)


def _dot_nt(a, b, **kw):
    return lax.dot_general(a, b, (((1,), (1,)), ((), ())), preferred_element_type=F32, **kw)


def _params(sem=None):
    return pltpu.CompilerParams(dimension_semantics=sem, vmem_limit_bytes=VMEM_LIMIT)


def _in_proj_kernel(x_ref, g_ref, w_ref, p_ref):
    x = x_ref[...]
    h = x * lax.rsqrt(jnp.mean(x * x, axis=-1, keepdims=True) + EPS) * g_ref[...]
    h = h.astype(BF16)
    for c in range(0, PW, PCH):
        p_ref[:, c:c + PCH] = _dot(h, w_ref[:, c:c + PCH])


def _in_proj(x2, g, w_cat, tm=256):
    n = x2.shape[0]
    return pl.pallas_call(
        _in_proj_kernel,
        grid=(n // tm,),
        in_specs=[pl.BlockSpec((tm, D), lambda i: (i, 0)),
                  pl.BlockSpec((1, D), lambda i: (0, 0)),
                  pl.BlockSpec((D, PW), lambda i: (0, 0))],
        out_specs=pl.BlockSpec((tm, PW), lambda i: (i, 0)),
        out_shape=jax.ShapeDtypeStruct((n, PW), F32),
        compiler_params=_params(("parallel",)),
        name="in_proj",
    )(x2, g.reshape(1, D), w_cat)


def _ret_kernel(q_ref, k_ref, v_ref, gt_ref, cos_ref, sin_ref, dmat_ref, vec_ref, cdec_ref, gn_ref,
                o_ref, run_ref, stash_ref):
    ph = pl.program_id(1)
    c = pl.program_id(2)
    nc = pl.num_programs(2)

    @pl.when(c == 0)
    def _():
        run_ref[...] = jnp.zeros_like(run_ref)

    cos2 = cos_ref[...]
    sinm = sin_ref[...]

    def rot(t):
        return t * cos2 + pltpu.roll(t, HD // 2, axis=1) * sinm

    def state_update(h, kh, vh, zeta, dec_row):
        kz = (kh * zeta).T.astype(BF16)
        contrib = _dot(kz, vh.astype(BF16))
        run_ref[h] = run_ref[h] * cdec_ref[dec_row:dec_row + 1, :] + contrib

    @pl.when(ph == 0)
    def _():
        ci = nc - 1 - c
        for h in range(NH):
            sl = slice(h * HD, (h + 1) * HD)
            kh = rot(k_ref[0, :, sl])
            stash_ref[ci, h] = run_ref[h]
            state_update(h, kh, v_ref[0, :, sl], vec_ref[3, h], NH + h)

    @pl.when(ph == 1)
    def _():
        for h in range(NH):
            sl = slice(h * HD, (h + 1) * HD)
            qh = rot(q_ref[0, :, sl]) * (HD ** -0.5)
            kh = rot(k_ref[0, :, sl])
            vh = v_ref[0, :, sl]
            vb = vh.astype(BF16)
            scores = _dot_nt(qh.astype(BF16), kh.astype(BF16)) * dmat_ref[h]
            o = _dot(scores.astype(BF16), vb)
            o += _dot((qh * vec_ref[0, h]).astype(BF16), run_ref[h].astype(BF16))
            o += _dot((qh * vec_ref[1, h]).astype(BF16), stash_ref[c, h].astype(BF16))
            state_update(h, kh, vh, vec_ref[2, h], h)
            mu = jnp.mean(o, axis=-1, keepdims=True)
            oc = o - mu
            var = jnp.mean(oc * oc, axis=-1, keepdims=True)
            on = oc * lax.rsqrt(var + EPS) * gn_ref[:, sl]
            gate = gt_ref[0, :, sl]
            o_ref[0, :, sl] = (on * (gate * jax.nn.sigmoid(gate))).astype(o_ref.dtype)


def _ret_tables(s):
    half = HD // 2
    inv_freq = 1.0 / (ROPE_BASE ** jnp.linspace(0.0, 1.0, half, dtype=F32))
    ang = jnp.arange(s, dtype=F32)[:, None] * inv_freq[None, :]
    cos, sin = jnp.cos(ang), jnp.sin(ang)
    cos2 = jnp.concatenate([cos, cos], axis=-1)
    sinm = jnp.concatenate([-sin, sin], axis=-1)
    lg = jnp.log(1.0 - 2.0 ** (-5.0 - jnp.arange(NH, dtype=F32)))
    lgb = lg[::-1]
    c = RET_C
    pos = jnp.arange(c, dtype=F32)
    diff = pos[:, None] - pos[None, :]
    dmat = jnp.where(diff >= 0, jnp.exp(lg[:, None, None] * jnp.maximum(diff, 0.0)),
                     jnp.exp(lgb[:, None, None] * jnp.maximum(-diff, 0.0)))
    xi_f = jnp.exp(lg[:, None] * (pos + 1.0))
    xi_b = jnp.exp(lgb[:, None] * (c - pos))
    zeta_f = jnp.exp(lg[:, None] * (c - 1 - pos))
    zeta_b = jnp.exp(lgb[:, None] * pos)
    vec = jnp.broadcast_to(jnp.stack([xi_f, xi_b, zeta_f, zeta_b])[..., None], (4, NH, c, LANES))
    cdec = jnp.concatenate([jnp.exp(lg * c), jnp.exp(lgb * c)])
    cdec = jnp.broadcast_to(cdec[:, None], (2 * NH, LANES))
    return cos2, sinm, dmat, vec, cdec


def _retention(p3, ret_gn):
    b, s, _ = p3.shape
    c = RET_C
    nc = s // c
    cos2, sinm, dmat, vec, cdec = _ret_tables(s)

    def chunk(ph, ci):
        return jnp.where(ph == 0, nc - 1 - ci, ci)

    def pspec(col):
        return pl.BlockSpec((1, c, HW), lambda bi, ph, ci: (bi, chunk(ph, ci), col))

    base = C_RET // HW
    return pl.pallas_call(
        _ret_kernel,
        grid=(b, 2, nc),
        in_specs=[pspec(base), pspec(base + 1), pspec(base + 2), pspec(base + 3),
                  pl.BlockSpec((c, HD), lambda bi, ph, ci: (chunk(ph, ci), 0)),
                  pl.BlockSpec((c, HD), lambda bi, ph, ci: (chunk(ph, ci), 0)),
                  pl.BlockSpec((NH, c, c), lambda bi, ph, ci: (0, 0, 0)),
                  pl.BlockSpec((4, NH, c, LANES), lambda bi, ph, ci: (0, 0, 0, 0)),
                  pl.BlockSpec((2 * NH, LANES), lambda bi, ph, ci: (0, 0)),
                  pl.BlockSpec((1, HW), lambda bi, ph, ci: (0, 0))],
        out_specs=pl.BlockSpec((1, c, HW), lambda bi, ph, ci: (bi, ci * ph, 0)),
        out_shape=jax.ShapeDtypeStruct((b, s, HW), BF16),
        scratch_shapes=[pltpu.VMEM((NH, HD, HD), F32), pltpu.VMEM((nc, NH, HD, HD), F32)],
        compiler_params=_params(("parallel", "arbitrary", "arbitrary")),
        name="retention",
    )(p3, p3, p3, p3, cos2, sinm, dmat, vec, cdec, ret_gn.reshape(1, HW))


def _gdn_prep_kernel(prev_ref, cur_ref, next_ref, sm_ref, cw_ref, alog_ref, dtb_ref, qkv_ref, smo_ref, *, tc):
    i = pl.program_id(1)
    last = pl.num_programs(1) - 1
    prev = jnp.where(i == 0, 0.0, prev_ref[0])
    nxt = jnp.where(i == last, 0.0, next_ref[0])
    ext = jnp.concatenate([prev, cur_ref[0], nxt], axis=0)
    nrow = tc + 16
    half = DN_CONV // 2
    y = jnp.zeros((tc, 3 * HW), F32)
    for j in range(DN_CONV):
        shifted = ext if j == half else pltpu.roll(ext, (half - j) % nrow, axis=0)
        y = y + shifted[8:8 + tc] * cw_ref[j:j + 1, :]
    y = y * jax.nn.sigmoid(y)
    for h in range(2 * NH):
        sl = slice(h * HD, (h + 1) * HD)
        t = y[:, sl]
        t = t * lax.rsqrt(jnp.sum(t * t, axis=-1, keepdims=True) + EPS)
        if h < NH:
            t = t * (HD ** -0.5)
        qkv_ref[0, :, sl] = t
    qkv_ref[0, :, 2 * HW:] = y[:, 2 * HW:]
    a = sm_ref[0]
    lane = lax.broadcasted_iota(jnp.int32, a.shape, 1)
    z = a + dtb_ref[...]
    softplus = jnp.maximum(z, 0.0) + jnp.log1p(jnp.exp(-jnp.abs(z)))
    g = -jnp.exp(alog_ref[...]) * softplus
    smo_ref[0] = jnp.where(lane < 2 * NH, jax.nn.sigmoid(a), g)


def _gdn_prep(p3, conv_w, a_log, dt_bias, tc=512):
    b, s, _ = p3.shape
    nt = s // tc
    r8 = tc // 8
    w3 = 3 * HW
    zeros8 = jnp.zeros((2 * NH,), F32)
    pad = jnp.zeros((LANES - 4 * NH,), F32)
    alog = jnp.concatenate([zeros8, a_log[0], a_log[1], pad]).reshape(1, LANES)
    dtb = jnp.concatenate([zeros8, dt_bias[0], dt_bias[1], pad]).reshape(1, LANES)
    return pl.pallas_call(
        functools.partial(_gdn_prep_kernel, tc=tc),
        grid=(b, nt),
        in_specs=[pl.BlockSpec((1, 8, w3), lambda bi, i: (bi, jnp.maximum(i * r8 - 1, 0), 0)),
                  pl.BlockSpec((1, tc, w3), lambda bi, i: (bi, i, 0)),
                  pl.BlockSpec((1, 8, w3), lambda bi, i: (bi, jnp.minimum((i + 1) * r8, s // 8 - 1), 0)),
                  pl.BlockSpec((1, tc, LANES), lambda bi, i: (bi, i, C_SMALL // LANES)),
                  pl.BlockSpec((DN_CONV, w3), lambda bi, i: (0, 0)),
                  pl.BlockSpec((1, LANES), lambda bi, i: (0, 0)),
                  pl.BlockSpec((1, LANES), lambda bi, i: (0, 0))],
        out_specs=[pl.BlockSpec((1, tc, w3), lambda bi, i: (bi, i, 0)),
                   pl.BlockSpec((1, tc, LANES), lambda bi, i: (bi, i, 0))],
        out_shape=[jax.ShapeDtypeStruct((b, s, w3), F32), jax.ShapeDtypeStruct((b, s, LANES), F32)],
        compiler_params=_params(("parallel", "parallel")),
        name="gdn_prep",
    )(p3, p3, p3, p3, conv_w, alog, dtb)


def _gdn_terms_kernel(qkv_ref, sm_ref, *out_refs, nchunk):
    c = DN_C
    row = lax.broadcasted_iota(jnp.int32, (c, c), 0)
    col = lax.broadcasted_iota(jnp.int32, (c, c), 1)
    lane = lax.broadcasted_iota(jnp.int32, (c, LANES), 1)
    gt_ref = out_refs[-1]
    ids, qn, kn, vv, gcum, beta, lmat, strict = [], [], [], [], [], [], [], []
    for ci in range(nchunk):
        rows = slice(ci * c, (ci + 1) * c)
        sm = sm_ref[0, rows, :]
        for d in range(2):
            tri = jnp.where((row <= col) if d else (row >= col), 1.0, 0.0)
            gc_all = _dot(tri, sm, precision=HIGHEST)
            gc_t = gc_all.T
            incl = (row <= col) if d else (row >= col)
            for h in range(NH):
                lg = 2 * NH + NH * d + h
                ids.append((ci, d, h))
                qn.append(qkv_ref[0, rows, h * HD:(h + 1) * HD])
                kn.append(qkv_ref[0, rows, HW + h * HD:HW + (h + 1) * HD])
                vv.append(qkv_ref[0, rows, 2 * HW + h * HD:2 * HW + (h + 1) * HD])
                beta.append(jnp.sum(jnp.where(lane == NH * d + h, sm, 0.0), axis=-1, keepdims=True))
                g = jnp.sum(jnp.where(lane == lg, gc_all, 0.0), axis=-1, keepdims=True)
                gcum.append(g)
                lmat.append(jnp.where(incl, jnp.exp(jnp.where(incl, g - gc_t[lg:lg + 1, :], 0.0)), 0.0))
                strict.append((row < col) if d else (row > col))
    n = len(ids)
    rng = range(n)
    kb = [kn[i] * beta[i] for i in rng]
    knb = [kn[i].astype(BF16) for i in rng]
    kk = [_dot_nt(kb[i].astype(BF16), knb[i]) for i in rng]
    a = [jnp.where(strict[i], kk[i] * lmat[i], 0.0) for i in rng]
    ab = [a[i].astype(BF16) for i in rng]
    p = [_dot(ab[i], ab[i]) for i in rng]
    m = [-a[i] for i in rng]
    steps = int(np.log2(c)) - 1
    for it in range(steps):
        pb = [p[i].astype(BF16) for i in rng]
        mp = [_dot(m[i].astype(BF16), pb[i]) for i in rng]
        m = [m[i] + p[i] + mp[i] for i in rng]
        if it + 1 < steps:
            p = [_dot(pb[i], pb[i]) for i in rng]
    eg = [jnp.exp(gcum[i]) for i in rng]
    rhs = [jnp.concatenate([vv[i] * beta[i], kb[i] * eg[i]], axis=-1) for i in rng]
    corr = [_dot(m[i].astype(BF16), rhs[i].astype(BF16)) for i in rng]
    attn = [_dot_nt(qn[i].astype(BF16), knb[i]) for i in rng]
    for i, (ci, d, h) in enumerate(ids):
        u_ref, w_ref, qd_ref, kdt_ref, att_ref = out_refs[5 * d:5 * d + 5]
        rows = slice(ci * c, (ci + 1) * c)
        sl = slice(h * HD, (h + 1) * HD)
        sol = rhs[i] + corr[i]
        gl = gcum[i][0:1, :] if d else gcum[i][c - 1:c, :]
        u_ref[0, rows, sl] = sol[:, :HD]
        w_ref[0, rows, sl] = sol[:, HD:].astype(BF16)
        qd_ref[0, rows, sl] = (qn[i] * eg[i]).astype(BF16)
        kdt_ref[0, ci, h] = (kn[i] * jnp.exp(gl - gcum[i])).T.astype(BF16)
        att_ref[0, ci, h] = (attn[i] * lmat[i]).astype(BF16)
        gt_ref[0, ci, NH * d + h:NH * d + h + 1, :] = jnp.broadcast_to(jnp.exp(gl), (1, LANES))


def _gdn_terms(qkv, sm, nchunk=2):
    b, s, w3 = qkv.shape
    c = DN_C
    nc = s // c
    r = nchunk * c
    tok = lambda dt: (pl.BlockSpec((1, r, HW), lambda bi, i: (bi, i, 0)), jax.ShapeDtypeStruct((b, s, HW), dt))
    kdt = (pl.BlockSpec((1, nchunk, NH, HD, c), lambda bi, i: (bi, i, 0, 0, 0)),
           jax.ShapeDtypeStruct((b, nc, NH, HD, c), BF16))
    att = (pl.BlockSpec((1, nchunk, NH, c, c), lambda bi, i: (bi, i, 0, 0, 0)),
           jax.ShapeDtypeStruct((b, nc, NH, c, c), BF16))
    gt = (pl.BlockSpec((1, nchunk, 2 * NH, LANES), lambda bi, i: (bi, i, 0, 0)),
          jax.ShapeDtypeStruct((b, nc, 2 * NH, LANES), F32))
    outs = [tok(F32), tok(BF16), tok(BF16), kdt, att] * 2 + [gt]
    return pl.pallas_call(
        functools.partial(_gdn_terms_kernel, nchunk=nchunk),
        grid=(b, nc // nchunk),
        in_specs=[pl.BlockSpec((1, r, w3), lambda bi, i: (bi, i, 0)),
                  pl.BlockSpec((1, r, LANES), lambda bi, i: (bi, i, 0))],
        out_specs=[o[0] for o in outs],
        out_shape=[o[1] for o in outs],
        compiler_params=_params(("parallel", "parallel")),
        name="gdn_terms",
    )(qkv, sm)


def _gdn_scan_kernel(*refs, nb):
    (uf, wf, qdf, kdtf, attf, gtf, ub, wb, qdb, kdtb, attb, gtb, of_ref, ob_ref, s_ref) = refs

    @pl.when(pl.program_id(1) == 0)
    def _():
        s_ref[...] = jnp.zeros_like(s_ref)

    dirs = ((uf, wf, qdf, kdtf, attf, gtf, of_ref), (ub, wb, qdb, kdtb, attb, gtb, ob_ref))
    ids = [(bi, d, h) for bi in range(nb) for d in range(2) for h in range(NH)]
    sl = [slice(h * HD, (h + 1) * HD) for h in range(NH)]
    s = [s_ref[bi, d, h] for bi, d, h in ids]
    sb = [t.astype(BF16) for t in s]
    ws = [_dot(dirs[d][1][bi, :, sl[h]], sb[i]) for i, (bi, d, h) in enumerate(ids)]
    qs = [_dot(dirs[d][2][bi, :, sl[h]], sb[i]) for i, (bi, d, h) in enumerate(ids)]
    vnb = [(dirs[d][0][bi, :, sl[h]] - ws[i]).astype(BF16) for i, (bi, d, h) in enumerate(ids)]
    av = [_dot(dirs[d][4][bi, 0, h], vnb[i]) for i, (bi, d, h) in enumerate(ids)]
    kv = [_dot(dirs[d][3][bi, 0, h], vnb[i]) for i, (bi, d, h) in enumerate(ids)]
    for i, (bi, d, h) in enumerate(ids):
        dirs[d][6][bi, :, sl[h]] = qs[i] + av[i]
        gt = dirs[d][5][bi, 0, NH * d + h:NH * d + h + 1, :]
        s_ref[bi, d, h] = s[i] * gt + kv[i]


def _gdn_scan(terms, nb):
    uf, wf, qdf, kdtf, attf, ub, wb, qdb, kdtb, attb, gt = terms
    b, s, _ = uf.shape
    c = DN_C
    nc = s // c
    fwd = lambda bi, i: i
    bwd = lambda bi, i: nc - 1 - i

    def specs(pos):
        return [pl.BlockSpec((nb, c, HW), lambda bi, i: (bi, pos(bi, i), 0)),
                pl.BlockSpec((nb, c, HW), lambda bi, i: (bi, pos(bi, i), 0)),
                pl.BlockSpec((nb, c, HW), lambda bi, i: (bi, pos(bi, i), 0)),
                pl.BlockSpec((nb, 1, NH, HD, c), lambda bi, i: (bi, pos(bi, i), 0, 0, 0)),
                pl.BlockSpec((nb, 1, NH, c, c), lambda bi, i: (bi, pos(bi, i), 0, 0, 0)),
                pl.BlockSpec((nb, 1, 2 * NH, LANES), lambda bi, i: (bi, pos(bi, i), 0, 0))]

    return pl.pallas_call(
        functools.partial(_gdn_scan_kernel, nb=nb),
        grid=(b // nb, nc),
        in_specs=specs(fwd) + specs(bwd),
        out_specs=[pl.BlockSpec((nb, c, HW), lambda bi, i: (bi, i, 0)),
                   pl.BlockSpec((nb, c, HW), lambda bi, i: (bi, nc - 1 - i, 0))],
        out_shape=[jax.ShapeDtypeStruct((b, s, HW), F32), jax.ShapeDtypeStruct((b, s, HW), F32)],
        scratch_shapes=[pltpu.VMEM((nb, 2, NH, HD, HD), F32)],
        compiler_params=_params(("parallel", "arbitrary")),
        name="gdn_scan",
    )(uf, wf, qdf, kdtf, attf, gt, ub, wb, qdb, kdtb, attb, gt)


def _merge_kernel(x_ref, oa_ref, of_ref, ob_ref, z_ref, ma_ref, mb_ref, ng_ref, wa_ref, wb_ref, wo_ref, x1_ref):
    ob_parts = []
    for h in range(NH):
        sl = slice(h * HD, (h + 1) * HD)
        o = of_ref[:, sl] + ob_ref[:, sl]
        o = o * lax.rsqrt(jnp.mean(o * o, axis=-1, keepdims=True) + EPS) * ng_ref[...]
        z = z_ref[:, sl]
        ob_parts.append((o * (z * jax.nn.sigmoid(z))).astype(BF16))
    ob = jnp.concatenate(ob_parts, axis=-1)
    ya = _dot(oa_ref[...], wa_ref[...])
    yb = _dot(ob, wb_ref[...])
    merged = jax.nn.sigmoid(ma_ref[...]) * ya + jax.nn.sigmoid(mb_ref[...]) * yb
    x1_ref[...] = x_ref[...] + _dot(merged.astype(BF16), wo_ref[...])


def _merge(x2, oa, o_f, o_b, p2, norm_g, wa, wb, wo, tm=512):
    n = x2.shape[0]
    const = lambda i: (0, 0)
    return pl.pallas_call(
        _merge_kernel,
        grid=(n // tm,),
        in_specs=[pl.BlockSpec((tm, D), lambda i: (i, 0)),
                  pl.BlockSpec((tm, HW), lambda i: (i, 0)),
                  pl.BlockSpec((tm, HW), lambda i: (i, 0)),
                  pl.BlockSpec((tm, HW), lambda i: (i, 0)),
                  pl.BlockSpec((tm, HW), lambda i: (i, C_Z // HW)),
                  pl.BlockSpec((tm, D), lambda i: (i, C_MA // D)),
                  pl.BlockSpec((tm, D), lambda i: (i, C_MB // D)),
                  pl.BlockSpec((1, HD), const),
                  pl.BlockSpec((HW, D), const),
                  pl.BlockSpec((HW, D), const),
                  pl.BlockSpec((D, D), const)],
        out_specs=pl.BlockSpec((tm, D), lambda i: (i, 0)),
        out_shape=jax.ShapeDtypeStruct((n, D), F32),
        compiler_params=_params(("parallel",)),
        name="merge",
    )(x2, oa, o_f, o_b, p2, p2, p2, norm_g.reshape(1, HD), wa, wb, wo)


def _top_rows(s, payload=None):
    r = s.shape[0]
    row = lax.broadcasted_iota(jnp.int32, s.shape, 0).astype(F32)
    vals, idxs = [], []
    for _ in range(P_TOPK):
        m = jnp.max(s, axis=0, keepdims=True)
        first = jnp.min(jnp.where(s == m, row, float(r)), axis=0, keepdims=True)
        sel = row == first
        vals.append(m)
        idxs.append(first if payload is None else jnp.max(jnp.where(sel, payload, -1.0), axis=0, keepdims=True))
        s = jnp.where(sel, -jnp.inf, s)
    return vals, idxs


def _stack_rows(rows):
    t = rows[0].shape[1]
    row = lax.broadcasted_iota(jnp.int32, (len(rows), t), 0)
    out = jnp.zeros((len(rows), t), F32)
    for i, r in enumerate(rows):
        out = jnp.where(row == i, r, out)
    return out


def _pair_candidates(v1, i1, v2, i2):
    v1s, i1s, v2s, i2s = _stack_rows(v1), _stack_rows(i1), _stack_rows(v2), _stack_rows(i2)
    brow = lax.broadcasted_iota(jnp.int32, (8, v2s.shape[1]), 0)
    cand = [v1[0] + v2s]
    cidx = [i1[0] * P_NKEYS + i2s]
    for a in range(1, 8):
        ok = brow < P_TOPK // (a + 1)
        cand.append(jnp.where(ok, v1[a] + v2s[0:8], -jnp.inf))
        cidx.append(i1[a] * P_NKEYS + i2s[0:8])
    cand.append(v1s[8:] + v2[0])
    cidx.append(i1s[8:] * P_NKEYS + i2[0])
    return jnp.concatenate(cand, axis=0), jnp.concatenate(cidx, axis=0)


def _route_kernel(x1_ref, g_ref, wq_ref, k1_ref, k2_ref, h2_ref, eidx_ref, gate_ref):
    x = x1_ref[...]
    h2 = x * lax.rsqrt(jnp.mean(x * x, axis=-1, keepdims=True) + EPS) * g_ref[...]
    h2_ref[...] = h2
    qp = _dot(h2.astype(BF16), wq_ref[...])
    gates, eidxs = [], []
    for h in range(P_HEADS):
        q1 = qp[:, h * P_DK:h * P_DK + P_DK // 2].astype(BF16)
        q2 = qp[:, h * P_DK + P_DK // 2:(h + 1) * P_DK].astype(BF16)
        s1 = _dot_nt(k1_ref[h], q1)
        s2 = _dot_nt(k2_ref[h], q2)
        v1, i1 = _top_rows(s1)
        v2, i2 = _top_rows(s2)
        cand, cidx = _pair_candidates(v1, i1, v2, i2)
        sc, ei = _top_rows(cand, cidx)
        ex = jnp.exp(_stack_rows(sc) - sc[0])
        gates.append(ex / jnp.sum(ex, axis=0, keepdims=True))
        eidxs.append((_stack_rows(ei) * float(SUB)).astype(jnp.int32))
    gate_ref[...] = jnp.concatenate(gates, axis=0).T
    eidx_ref[...] = jnp.concatenate(eidxs, axis=0).T


def _route(x1, g_ffn, wq, k1, k2, tm=128):
    n = x1.shape[0]
    return pl.pallas_call(
        _route_kernel,
        grid=(n // tm,),
        in_specs=[pl.BlockSpec((tm, D), lambda i: (i, 0)),
                  pl.BlockSpec((1, D), lambda i: (0, 0)),
                  pl.BlockSpec((D, P_HEADS * P_DK), lambda i: (0, 0)),
                  pl.BlockSpec((P_HEADS, P_NKEYS, P_DK // 2), lambda i: (0, 0, 0)),
                  pl.BlockSpec((P_HEADS, P_NKEYS, P_DK // 2), lambda i: (0, 0, 0))],
        out_specs=[pl.BlockSpec((tm, D), lambda i: (i, 0)),
                   pl.BlockSpec((tm, P_NSEL), lambda i: (i, 0)),
                   pl.BlockSpec((tm, P_NSEL), lambda i: (i, 0))],
        out_shape=[jax.ShapeDtypeStruct((n, D), F32),
                   jax.ShapeDtypeStruct((n, P_NSEL), jnp.int32),
                   jax.ShapeDtypeStruct((n, P_NSEL), F32)],
        compiler_params=_params(("parallel",)),
        name="route",
    )(x1, g_ffn.reshape(1, D), wq, k1, k2)


def _pack_table(t):
    bits = lax.bitcast_convert_type(t.astype(BF16), jnp.uint16).astype(jnp.uint32)
    packed = bits[:, :D // 2] | (bits[:, D // 2:] << 16)
    return lax.bitcast_convert_type(packed, jnp.int32).reshape(t.shape[0] * SUB, LANES)


PEER_UNROLL = 32
NROWS = P_NSEL * SUB
NCOL = 2 * NROWS


OFF_WAYS = 16
OFF_PER = P_NSEL // OFF_WAYS


def _deal_offsets(eoff):
    n = eoff.shape[0]
    return [eoff[:, OFF_PER * j:OFF_PER * (j + 1)].reshape(n * OFF_PER) for j in range(OFF_WAYS)]


def _gather_rows(off_refs, tab_ref, g_ref, t):
    off = 0
    for i in range(OFF_PER):
        pos = t * OFF_PER + i
        for j, off_ref in enumerate(off_refs):
            k = OFF_PER * j + i
            off = pl.multiple_of(off_ref[pos], SUB)
            g_ref[SUB * k:SUB * (k + 1), :] = tab_ref[pl.ds(off, SUB), :]
    return off


def _two_buffer_loop(tm, gather, compute, ga_ref, gb_ref):
    gather(0, ga_ref)
    bufs = (ga_ref, gb_ref)

    def body(i, carry):
        t = PEER_UNROLL * i
        last = 0
        for j in range(PEER_UNROLL):
            compute(t + j, bufs[j % 2])
            nxt = jnp.minimum(t + j + 1, tm - 1) + jnp.minimum(last, 0)
            last = gather(nxt, bufs[(j + 1) % 2])
        return carry

    lax.fori_loop(0, tm // PEER_UNROLL, body, 0)


def _diag_mask():
    j = lax.broadcasted_iota(jnp.int32, (2 * SUB, NCOL), 0)
    c = lax.broadcasted_iota(jnp.int32, (2 * SUB, NCOL), 1)
    return (c & 7) == ((j & 3) * 2 + (j >> 2))


def _split_bf16(a):
    hi = a.astype(BF16)
    lo = (a - hi.astype(F32)).astype(BF16)
    return jnp.concatenate([hi, lo], axis=0)


def _peer_u_kernel(*refs, tm):
    off_refs = refs[:OFF_WAYS]
    x_ref, gate_ref, tab_ref, sel_ref, w_ref, ga_ref, gb_ref, r_ref = refs[OFF_WAYS:]
    mask = _diag_mask()

    def gather(t, g_ref):
        return _gather_rows(off_refs, tab_ref, g_ref, t)

    def compute(t, g_ref):
        g16 = pltpu.bitcast(g_ref[...], BF16)
        x8 = x_ref[pl.ds(t, 1), :].reshape(2 * SUB, LANES)
        o = _dot_nt(_split_bf16(x8), g16)
        o8 = o[0:2 * SUB] + o[2 * SUB:]
        r_ref[pl.ds(t, 1), :] = jnp.sum(jnp.where(mask, o8, 0.0), axis=0, keepdims=True)

    _two_buffer_loop(tm, gather, compute, ga_ref, gb_ref)
    pre = _dot(r_ref[...], sel_ref[...], precision=HIGHEST)
    act = 0.5 * pre * (1.0 + lax.erf(pre * (2.0 ** -0.5)))
    w_ref[...] = gate_ref[...] * act


def _group_matrix():
    c = jnp.arange(NCOL, dtype=jnp.int32)[:, None] // 8
    k = jnp.arange(P_NSEL, dtype=jnp.int32)[None, :]
    return (c == k).astype(F32)


def _off_specs(tm):
    return [pl.BlockSpec((tm * OFF_PER,), lambda i: (i,), memory_space=pltpu.SMEM) for _ in range(OFF_WAYS)]


def _peer_u(eoff, h2r, gate, tab, tm=256):
    n = eoff.shape[0]
    return pl.pallas_call(
        functools.partial(_peer_u_kernel, tm=tm),
        grid=(n // tm,),
        in_specs=_off_specs(tm) + [
                  pl.BlockSpec((tm, D), lambda i: (i, 0)),
                  pl.BlockSpec((tm, P_NSEL), lambda i: (i, 0)),
                  pl.BlockSpec(memory_space=pltpu.VMEM),
                  pl.BlockSpec((NCOL, P_NSEL), lambda i: (0, 0))],
        out_specs=pl.BlockSpec((tm, P_NSEL), lambda i: (i, 0)),
        out_shape=jax.ShapeDtypeStruct((n, P_NSEL), F32),
        scratch_shapes=[pltpu.VMEM((NROWS, LANES), jnp.int32), pltpu.VMEM((NROWS, LANES), jnp.int32),
                        pltpu.VMEM((tm, NCOL), F32)],
        compiler_params=_params(("arbitrary",)),
        name="peer_u",
    )(*_deal_offsets(eoff), h2r, gate, tab, _group_matrix())


def _peer_v_kernel(*refs, tm):
    off_refs = refs[:OFF_WAYS]
    w_ref, tab_ref, rep_ref, x1_ref, gf_ref, y_ref, ga_ref, gb_ref, w8_ref, po_ref = refs[OFF_WAYS:]
    mask = _diag_mask()
    w8_ref[...] = _dot(w_ref[...], rep_ref[...], precision=HIGHEST)
    row8 = lax.broadcasted_iota(jnp.int32, (2 * SUB, LANES), 0)

    def gather(t, g_ref):
        return _gather_rows(off_refs, tab_ref, g_ref, t)

    def compute(t, g_ref):
        g16 = pltpu.bitcast(g_ref[...], BF16)
        wexp = jnp.where(mask, jnp.broadcast_to(w8_ref[pl.ds(t, 1), :], (2 * SUB, NCOL)), 0.0)
        o = _dot(_split_bf16(wexp), g16)
        o8 = o[0:2 * SUB] + o[2 * SUB:]
        pieces = [jnp.sum(jnp.where(row8 == j, o8, 0.0), axis=0, keepdims=True) for j in range(2 * SUB)]
        po_ref[pl.ds(t, 1), :] = jnp.concatenate(pieces, axis=1)

    _two_buffer_loop(tm, gather, compute, ga_ref, gb_ref)
    x = x1_ref[...] + po_ref[...]
    y_ref[...] = x * lax.rsqrt(jnp.mean(x * x, axis=-1, keepdims=True) + EPS) * gf_ref[...]


def _peer_v(eoff, w, tab, x1, g_final, tm=256):
    n = eoff.shape[0]
    return pl.pallas_call(
        functools.partial(_peer_v_kernel, tm=tm),
        grid=(n // tm,),
        in_specs=_off_specs(tm) + [
                  pl.BlockSpec((tm, P_NSEL), lambda i: (i, 0)),
                  pl.BlockSpec(memory_space=pltpu.VMEM),
                  pl.BlockSpec((P_NSEL, NCOL), lambda i: (0, 0)),
                  pl.BlockSpec((tm, D), lambda i: (i, 0)),
                  pl.BlockSpec((1, D), lambda i: (0, 0))],
        out_specs=pl.BlockSpec((tm, D), lambda i: (i, 0)),
        out_shape=jax.ShapeDtypeStruct((n, D), F32),
        scratch_shapes=[pltpu.VMEM((NROWS, LANES), jnp.int32), pltpu.VMEM((NROWS, LANES), jnp.int32),
                        pltpu.VMEM((tm, NCOL), F32), pltpu.VMEM((tm, D), F32)],
        compiler_params=_params(("arbitrary",)),
        name="peer_v",
    )(*_deal_offsets(eoff), w, tab, _group_matrix().T, x1, g_final.reshape(1, D))


def _pack_w_in(w_in):
    w = jnp.concatenate([w_in[:, 2048:3584], w_in[:, :2048], w_in[:, 3584:4096], w_in[:, 4112:6160],
                         w_in[:, 4096:4112], jnp.zeros((D, PW - C_SMALL - 16), w_in.dtype)], axis=1)
    return w.astype(BF16)


def _encoder(x, wts):
    b, s, _ = x.shape
    n = b * s
    x2 = x.reshape(n, D)
    p2 = _in_proj(x2, wts["g_mix"], wts["w_cat"])
    p3 = p2.reshape(b, s, PW)
    oa = _retention(p3, wts["ret_gn"]).reshape(n, HW)
    qkv, sm = _gdn_prep(p3, wts["dn_conv"], wts["dn_a_log"], wts["dn_dt_bias"])
    o_f, o_b = _gdn_scan(_gdn_terms(qkv, sm), nb=4 if b % 4 == 0 else 1)
    x1 = _merge(x2, oa, o_f.reshape(n, HW), o_b.reshape(n, HW), p2, wts["dn_norm"],
                wts["wa"], wts["wb"], wts["wo"])
    h2, eidx, gate = _route(x1, wts["g_ffn"], wts["wq"], wts["k1"], wts["k2"])
    w = _peer_u(eidx, h2, gate, wts["tab_u"])
    return _peer_v(eidx, w, wts["tab_v"], x1, wts["g_final"]).reshape(b, s, D)


def kernel(x_prompt, x_sample, g_mix, w_in, ret_gn, dn_conv, dn_a_log, dn_dt_bias, dn_norm, w_branch_a, w_branch_b, w_out, g_ffn, peer_wq, peer_k1, peer_k2, peer_u, peer_v, g_final):
    wts = dict(
        g_mix=g_mix[0], w_cat=_pack_w_in(w_in[0]), ret_gn=ret_gn[0], dn_conv=dn_conv[0],
        dn_a_log=dn_a_log[0], dn_dt_bias=dn_dt_bias[0], dn_norm=dn_norm[0],
        wa=w_branch_a[0].astype(BF16), wb=w_branch_b[0].astype(BF16), wo=w_out[0].astype(BF16),
        g_ffn=g_ffn[0], wq=peer_wq[0].astype(BF16), k1=peer_k1[0].astype(BF16), k2=peer_k2[0].astype(BF16),
        tab_u=_pack_table(peer_u[0]), tab_v=_pack_table(peer_v[0]), g_final=g_final)
    return (_encoder(x_prompt, wts), _encoder(x_sample, wts))
```

```python
import functools

import jax
import jax.numpy as jnp
import numpy as np
from jax import lax
from jax.experimental import pallas as pl
from jax.experimental.pallas import tpu as pltpu

F32 = jnp.float32
BF16 = jnp.bfloat16
HIGHEST = lax.Precision.HIGHEST

D = 1024
EPS = 1e-6
NH = 4
HD = 128
HW = NH * HD
ROPE_BASE = 10000.0
DN_CONV = 5
DN_C = 64
RET_C = 512
P_HEADS = 8
P_DK = 256
P_NKEYS = 128
P_TOPK = 16
P_NSEL = P_HEADS * P_TOPK
LANES = 128
SUB = 4
VMEM_LIMIT = 56 * 1024 * 1024

C_DNQKV = 0
C_RET = 1536
C_Z = 3584
C_MA = 4096
C_MB = 5120
C_SMALL = 6144
PW = 6272
PCH = 896


def _dot(a, b, **kw):
    return jnp.dot(a, b, preferred_element_type=F32, **kw)


def _dot_nt(a, b, **kw):
    return lax.dot_general(a, b, (((1,), (1,)), ((), ())), preferred_element_type=F32, **kw)


def _params(sem=None):
    return pltpu.CompilerParams(dimension_semantics=sem, vmem_limit_bytes=VMEM_LIMIT)


def _in_proj_kernel(x_ref, g_ref, w_ref, p_ref):
    x = x_ref[...]
    h = x * lax.rsqrt(jnp.mean(x * x, axis=-1, keepdims=True) + EPS) * g_ref[...]
    h = h.astype(BF16)
    for c in range(0, PW, PCH):
        p_ref[:, c:c + PCH] = _dot(h, w_ref[:, c:c + PCH])


def _in_proj(x2, g, w_cat, tm=256):
    n = x2.shape[0]
    return pl.pallas_call(
        _in_proj_kernel,
        grid=(n // tm,),
        in_specs=[pl.BlockSpec((tm, D), lambda i: (i, 0)),
                  pl.BlockSpec((1, D), lambda i: (0, 0)),
                  pl.BlockSpec((D, PW), lambda i: (0, 0))],
        out_specs=pl.BlockSpec((tm, PW), lambda i: (i, 0)),
        out_shape=jax.ShapeDtypeStruct((n, PW), F32),
        compiler_params=_params(("parallel",)),
        name="in_proj",
    )(x2, g.reshape(1, D), w_cat)


def _ret_kernel(q_ref, k_ref, v_ref, gt_ref, cos_ref, sin_ref, dmat_ref, vec_ref, cdec_ref, gn_ref,
                o_ref, run_ref, stash_ref):
    ph = pl.program_id(1)
    c = pl.program_id(2)
    nc = pl.num_programs(2)

    @pl.when(c == 0)
    def _():
        run_ref[...] = jnp.zeros_like(run_ref)

    cos2 = cos_ref[...]
    sinm = sin_ref[...]

    def rot(t):
        return t * cos2 + pltpu.roll(t, HD // 2, axis=1) * sinm

    def state_update(h, kh, vh, zeta, dec_row):
        kz = (kh * zeta).T.astype(BF16)
        contrib = _dot(kz, vh.astype(BF16))
        run_ref[h] = run_ref[h] * cdec_ref[dec_row:dec_row + 1, :] + contrib

    @pl.when(ph == 0)
    def _():
        ci = nc - 1 - c
        for h in range(NH):
            sl = slice(h * HD, (h + 1) * HD)
            kh = rot(k_ref[0, :, sl])
            stash_ref[ci, h] = run_ref[h]
            state_update(h, kh, v_ref[0, :, sl], vec_ref[3, h], NH + h)

    @pl.when(ph == 1)
    def _():
        for h in range(NH):
            sl = slice(h * HD, (h + 1) * HD)
            qh = rot(q_ref[0, :, sl]) * (HD ** -0.5)
            kh = rot(k_ref[0, :, sl])
            vh = v_ref[0, :, sl]
            vb = vh.astype(BF16)
            scores = _dot_nt(qh.astype(BF16), kh.astype(BF16)) * dmat_ref[h]
            o = _dot(scores.astype(BF16), vb)
            o += _dot((qh * vec_ref[0, h]).astype(BF16), run_ref[h].astype(BF16))
            o += _dot((qh * vec_ref[1, h]).astype(BF16), stash_ref[c, h].astype(BF16))
            state_update(h, kh, vh, vec_ref[2, h], h)
            mu = jnp.mean(o, axis=-1, keepdims=True)
            oc = o - mu
            var = jnp.mean(oc * oc, axis=-1, keepdims=True)
            on = oc * lax.rsqrt(var + EPS) * gn_ref[:, sl]
            gate = gt_ref[0, :, sl]
            o_ref[0, :, sl] = (on * (gate * jax.nn.sigmoid(gate))).astype(o_ref.dtype)


def _ret_tables(s):
    half = HD // 2
    inv_freq = 1.0 / (ROPE_BASE ** jnp.linspace(0.0, 1.0, half, dtype=F32))
    ang = jnp.arange(s, dtype=F32)[:, None] * inv_freq[None, :]
    cos, sin = jnp.cos(ang), jnp.sin(ang)
    cos2 = jnp.concatenate([cos, cos], axis=-1)
    sinm = jnp.concatenate([-sin, sin], axis=-1)
    lg = jnp.log(1.0 - 2.0 ** (-5.0 - jnp.arange(NH, dtype=F32)))
    lgb = lg[::-1]
    c = RET_C
    pos = jnp.arange(c, dtype=F32)
    diff = pos[:, None] - pos[None, :]
    dmat = jnp.where(diff >= 0, jnp.exp(lg[:, None, None] * jnp.maximum(diff, 0.0)),
                     jnp.exp(lgb[:, None, None] * jnp.maximum(-diff, 0.0)))
    xi_f = jnp.exp(lg[:, None] * (pos + 1.0))
    xi_b = jnp.exp(lgb[:, None] * (c - pos))
    zeta_f = jnp.exp(lg[:, None] * (c - 1 - pos))
    zeta_b = jnp.exp(lgb[:, None] * pos)
    vec = jnp.broadcast_to(jnp.stack([xi_f, xi_b, zeta_f, zeta_b])[..., None], (4, NH, c, LANES))
    cdec = jnp.concatenate([jnp.exp(lg * c), jnp.exp(lgb * c)])
    cdec = jnp.broadcast_to(cdec[:, None], (2 * NH, LANES))
    return cos2, sinm, dmat, vec, cdec


def _retention(p3, ret_gn):
    b, s, _ = p3.shape
    c = RET_C
    nc = s // c
    cos2, sinm, dmat, vec, cdec = _ret_tables(s)

    def chunk(ph, ci):
        return jnp.where(ph == 0, nc - 1 - ci, ci)

    def pspec(col):
        return pl.BlockSpec((1, c, HW), lambda bi, ph, ci: (bi, chunk(ph, ci), col))

    base = C_RET // HW
    return pl.pallas_call(
        _ret_kernel,
        grid=(b, 2, nc),
        in_specs=[pspec(base), pspec(base + 1), pspec(base + 2), pspec(base + 3),
                  pl.BlockSpec((c, HD), lambda bi, ph, ci: (chunk(ph, ci), 0)),
                  pl.BlockSpec((c, HD), lambda bi, ph, ci: (chunk(ph, ci), 0)),
                  pl.BlockSpec((NH, c, c), lambda bi, ph, ci: (0, 0, 0)),
                  pl.BlockSpec((4, NH, c, LANES), lambda bi, ph, ci: (0, 0, 0, 0)),
                  pl.BlockSpec((2 * NH, LANES), lambda bi, ph, ci: (0, 0)),
                  pl.BlockSpec((1, HW), lambda bi, ph, ci: (0, 0))],
        out_specs=pl.BlockSpec((1, c, HW), lambda bi, ph, ci: (bi, ci * ph, 0)),
        out_shape=jax.ShapeDtypeStruct((b, s, HW), BF16),
        scratch_shapes=[pltpu.VMEM((NH, HD, HD), F32), pltpu.VMEM((nc, NH, HD, HD), F32)],
        compiler_params=_params(("parallel", "arbitrary", "arbitrary")),
        name="retention",
    )(p3, p3, p3, p3, cos2, sinm, dmat, vec, cdec, ret_gn.reshape(1, HW))


def _gdn_prep_kernel(prev_ref, cur_ref, next_ref, sm_ref, cw_ref, alog_ref, dtb_ref, qkv_ref, smo_ref, *, tc):
    i = pl.program_id(1)
    last = pl.num_programs(1) - 1
    prev = jnp.where(i == 0, 0.0, prev_ref[0])
    nxt = jnp.where(i == last, 0.0, next_ref[0])
    ext = jnp.concatenate([prev, cur_ref[0], nxt], axis=0)
    nrow = tc + 16
    half = DN_CONV // 2
    y = jnp.zeros((tc, 3 * HW), F32)
    for j in range(DN_CONV):
        shifted = ext if j == half else pltpu.roll(ext, (half - j) % nrow, axis=0)
        y = y + shifted[8:8 + tc] * cw_ref[j:j + 1, :]
    y = y * jax.nn.sigmoid(y)
    for h in range(2 * NH):
        sl = slice(h * HD, (h + 1) * HD)
        t = y[:, sl]
        t = t * lax.rsqrt(jnp.sum(t * t, axis=-1, keepdims=True) + EPS)
        if h < NH:
            t = t * (HD ** -0.5)
        qkv_ref[0, :, sl] = t
    qkv_ref[0, :, 2 * HW:] = y[:, 2 * HW:]
    a = sm_ref[0]
    lane = lax.broadcasted_iota(jnp.int32, a.shape, 1)
    z = a + dtb_ref[...]
    softplus = jnp.maximum(z, 0.0) + jnp.log1p(jnp.exp(-jnp.abs(z)))
    g = -jnp.exp(alog_ref[...]) * softplus
    smo_ref[0] = jnp.where(lane < 2 * NH, jax.nn.sigmoid(a), g)


def _gdn_prep(p3, conv_w, a_log, dt_bias, tc=512):
    b, s, _ = p3.shape
    nt = s // tc
    r8 = tc // 8
    w3 = 3 * HW
    zeros8 = jnp.zeros((2 * NH,), F32)
    pad = jnp.zeros((LANES - 4 * NH,), F32)
    alog = jnp.concatenate([zeros8, a_log[0], a_log[1], pad]).reshape(1, LANES)
    dtb = jnp.concatenate([zeros8, dt_bias[0], dt_bias[1], pad]).reshape(1, LANES)
    return pl.pallas_call(
        functools.partial(_gdn_prep_kernel, tc=tc),
        grid=(b, nt),
        in_specs=[pl.BlockSpec((1, 8, w3), lambda bi, i: (bi, jnp.maximum(i * r8 - 1, 0), 0)),
                  pl.BlockSpec((1, tc, w3), lambda bi, i: (bi, i, 0)),
                  pl.BlockSpec((1, 8, w3), lambda bi, i: (bi, jnp.minimum((i + 1) * r8, s // 8 - 1), 0)),
                  pl.BlockSpec((1, tc, LANES), lambda bi, i: (bi, i, C_SMALL // LANES)),
                  pl.BlockSpec((DN_CONV, w3), lambda bi, i: (0, 0)),
                  pl.BlockSpec((1, LANES), lambda bi, i: (0, 0)),
                  pl.BlockSpec((1, LANES), lambda bi, i: (0, 0))],
        out_specs=[pl.BlockSpec((1, tc, w3), lambda bi, i: (bi, i, 0)),
                   pl.BlockSpec((1, tc, LANES), lambda bi, i: (bi, i, 0))],
        out_shape=[jax.ShapeDtypeStruct((b, s, w3), F32), jax.ShapeDtypeStruct((b, s, LANES), F32)],
        compiler_params=_params(("parallel", "parallel")),
        name="gdn_prep",
    )(p3, p3, p3, p3, conv_w, alog, dtb)


def _gdn_terms_kernel(qkv_ref, sm_ref, *out_refs, nchunk):
    c = DN_C
    row = lax.broadcasted_iota(jnp.int32, (c, c), 0)
    col = lax.broadcasted_iota(jnp.int32, (c, c), 1)
    lane = lax.broadcasted_iota(jnp.int32, (c, LANES), 1)
    gt_ref = out_refs[-1]
    ids, qn, kn, vv, gcum, beta, lmat, strict = [], [], [], [], [], [], [], []
    for ci in range(nchunk):
        rows = slice(ci * c, (ci + 1) * c)
        sm = sm_ref[0, rows, :]
        for d in range(2):
            tri = jnp.where((row <= col) if d else (row >= col), 1.0, 0.0)
            gc_all = _dot(tri, sm, precision=HIGHEST)
            gc_t = gc_all.T
            incl = (row <= col) if d else (row >= col)
            for h in range(NH):
                lg = 2 * NH + NH * d + h
                ids.append((ci, d, h))
                qn.append(qkv_ref[0, rows, h * HD:(h + 1) * HD])
                kn.append(qkv_ref[0, rows, HW + h * HD:HW + (h + 1) * HD])
                vv.append(qkv_ref[0, rows, 2 * HW + h * HD:2 * HW + (h + 1) * HD])
                beta.append(jnp.sum(jnp.where(lane == NH * d + h, sm, 0.0), axis=-1, keepdims=True))
                g = jnp.sum(jnp.where(lane == lg, gc_all, 0.0), axis=-1, keepdims=True)
                gcum.append(g)
                lmat.append(jnp.where(incl, jnp.exp(jnp.where(incl, g - gc_t[lg:lg + 1, :], 0.0)), 0.0))
                strict.append((row < col) if d else (row > col))
    n = len(ids)
    rng = range(n)
    kb = [kn[i] * beta[i] for i in rng]
    knb = [kn[i].astype(BF16) for i in rng]
    kk = [_dot_nt(kb[i].astype(BF16), knb[i]) for i in rng]
    a = [jnp.where(strict[i], kk[i] * lmat[i], 0.0) for i in rng]
    ab = [a[i].astype(BF16) for i in rng]
    p = [_dot(ab[i], ab[i]) for i in rng]
    m = [-a[i] for i in rng]
    steps = int(np.log2(c)) - 1
    for it in range(steps):
        pb = [p[i].astype(BF16) for i in rng]
        mp = [_dot(m[i].astype(BF16), pb[i]) for i in rng]
        m = [m[i] + p[i] + mp[i] for i in rng]
        if it + 1 < steps:
            p = [_dot(pb[i], pb[i]) for i in rng]
    eg = [jnp.exp(gcum[i]) for i in rng]
    rhs = [jnp.concatenate([vv[i] * beta[i], kb[i] * eg[i]], axis=-1) for i in rng]
    corr = [_dot(m[i].astype(BF16), rhs[i].astype(BF16)) for i in rng]
    attn = [_dot_nt(qn[i].astype(BF16), knb[i]) for i in rng]
    for i, (ci, d, h) in enumerate(ids):
        u_ref, w_ref, qd_ref, kdt_ref, att_ref = out_refs[5 * d:5 * d + 5]
        rows = slice(ci * c, (ci + 1) * c)
        sl = slice(h * HD, (h + 1) * HD)
        sol = rhs[i] + corr[i]
        gl = gcum[i][0:1, :] if d else gcum[i][c - 1:c, :]
        u_ref[0, rows, sl] = sol[:, :HD]
        w_ref[0, rows, sl] = sol[:, HD:].astype(BF16)
        qd_ref[0, rows, sl] = (qn[i] * eg[i]).astype(BF16)
        kdt_ref[0, ci, h] = (kn[i] * jnp.exp(gl - gcum[i])).T.astype(BF16)
        att_ref[0, ci, h] = (attn[i] * lmat[i]).astype(BF16)
        gt_ref[0, ci, NH * d + h:NH * d + h + 1, :] = jnp.broadcast_to(jnp.exp(gl), (1, LANES))


def _gdn_terms(qkv, sm, nchunk=4):
    b, s, w3 = qkv.shape
    c = DN_C
    nc = s // c
    r = nchunk * c
    tok = lambda dt: (pl.BlockSpec((1, r, HW), lambda bi, i: (bi, i, 0)), jax.ShapeDtypeStruct((b, s, HW), dt))
    kdt = (pl.BlockSpec((1, nchunk, NH, HD, c), lambda bi, i: (bi, i, 0, 0, 0)),
           jax.ShapeDtypeStruct((b, nc, NH, HD, c), BF16))
    att = (pl.BlockSpec((1, nchunk, NH, c, c), lambda bi, i: (bi, i, 0, 0, 0)),
           jax.ShapeDtypeStruct((b, nc, NH, c, c), BF16))
    gt = (pl.BlockSpec((1, nchunk, 2 * NH, LANES), lambda bi, i: (bi, i, 0, 0)),
          jax.ShapeDtypeStruct((b, nc, 2 * NH, LANES), F32))
    outs = [tok(F32), tok(BF16), tok(BF16), kdt, att] * 2 + [gt]
    return pl.pallas_call(
        functools.partial(_gdn_terms_kernel, nchunk=nchunk),
        grid=(b, nc // nchunk),
        in_specs=[pl.BlockSpec((1, r, w3), lambda bi, i: (bi, i, 0)),
                  pl.BlockSpec((1, r, LANES), lambda bi, i: (bi, i, 0))],
        out_specs=[o[0] for o in outs],
        out_shape=[o[1] for o in outs],
        compiler_params=_params(("parallel", "parallel")),
        name="gdn_terms",
    )(qkv, sm)


def _gdn_scan_kernel(*refs, nb):
    (uf, wf, qdf, kdtf, attf, gtf, ub, wb, qdb, kdtb, attb, gtb, of_ref, ob_ref, s_ref) = refs

    @pl.when(pl.program_id(1) == 0)
    def _():
        s_ref[...] = jnp.zeros_like(s_ref)

    dirs = ((uf, wf, qdf, kdtf, attf, gtf, of_ref), (ub, wb, qdb, kdtb, attb, gtb, ob_ref))
    ids = [(bi, d, h) for bi in range(nb) for d in range(2) for h in range(NH)]
    sl = [slice(h * HD, (h + 1) * HD) for h in range(NH)]
    s = [s_ref[bi, d, h] for bi, d, h in ids]
    sb = [t.astype(BF16) for t in s]
    ws = [_dot(dirs[d][1][bi, :, sl[h]], sb[i]) for i, (bi, d, h) in enumerate(ids)]
    qs = [_dot(dirs[d][2][bi, :, sl[h]], sb[i]) for i, (bi, d, h) in enumerate(ids)]
    vnb = [(dirs[d][0][bi, :, sl[h]] - ws[i]).astype(BF16) for i, (bi, d, h) in enumerate(ids)]
    av = [_dot(dirs[d][4][bi, 0, h], vnb[i]) for i, (bi, d, h) in enumerate(ids)]
    kv = [_dot(dirs[d][3][bi, 0, h], vnb[i]) for i, (bi, d, h) in enumerate(ids)]
    for i, (bi, d, h) in enumerate(ids):
        dirs[d][6][bi, :, sl[h]] = qs[i] + av[i]
        gt = dirs[d][5][bi, 0, NH * d + h:NH * d + h + 1, :]
        s_ref[bi, d, h] = s[i] * gt + kv[i]


def _gdn_scan(terms, nb):
    uf, wf, qdf, kdtf, attf, ub, wb, qdb, kdtb, attb, gt = terms
    b, s, _ = uf.shape
    c = DN_C
    nc = s // c
    fwd = lambda bi, i: i
    bwd = lambda bi, i: nc - 1 - i

    def specs(pos):
        return [pl.BlockSpec((nb, c, HW), lambda bi, i: (bi, pos(bi, i), 0)),
                pl.BlockSpec((nb, c, HW), lambda bi, i: (bi, pos(bi, i), 0)),
                pl.BlockSpec((nb, c, HW), lambda bi, i: (bi, pos(bi, i), 0)),
                pl.BlockSpec((nb, 1, NH, HD, c), lambda bi, i: (bi, pos(bi, i), 0, 0, 0)),
                pl.BlockSpec((nb, 1, NH, c, c), lambda bi, i: (bi, pos(bi, i), 0, 0, 0)),
                pl.BlockSpec((nb, 1, 2 * NH, LANES), lambda bi, i: (bi, pos(bi, i), 0, 0))]

    return pl.pallas_call(
        functools.partial(_gdn_scan_kernel, nb=nb),
        grid=(b // nb, nc),
        in_specs=specs(fwd) + specs(bwd),
        out_specs=[pl.BlockSpec((nb, c, HW), lambda bi, i: (bi, i, 0)),
                   pl.BlockSpec((nb, c, HW), lambda bi, i: (bi, nc - 1 - i, 0))],
        out_shape=[jax.ShapeDtypeStruct((b, s, HW), F32), jax.ShapeDtypeStruct((b, s, HW), F32)],
        scratch_shapes=[pltpu.VMEM((nb, 2, NH, HD, HD), F32)],
        compiler_params=_params(("parallel", "arbitrary")),
        name="gdn_scan",
    )(uf, wf, qdf, kdtf, attf, gt, ub, wb, qdb, kdtb, attb, gt)


def _merge_kernel(x_ref, oa_ref, of_ref, ob_ref, z_ref, ma_ref, mb_ref, ng_ref, wa_ref, wb_ref, wo_ref, x1_ref):
    ob_parts = []
    for h in range(NH):
        sl = slice(h * HD, (h + 1) * HD)
        o = of_ref[:, sl] + ob_ref[:, sl]
        o = o * lax.rsqrt(jnp.mean(o * o, axis=-1, keepdims=True) + EPS) * ng_ref[...]
        z = z_ref[:, sl]
        ob_parts.append((o * (z * jax.nn.sigmoid(z))).astype(BF16))
    ob = jnp.concatenate(ob_parts, axis=-1)
    ya = _dot(oa_ref[...], wa_ref[...])
    yb = _dot(ob, wb_ref[...])
    merged = jax.nn.sigmoid(ma_ref[...]) * ya + jax.nn.sigmoid(mb_ref[...]) * yb
    x1_ref[...] = x_ref[...] + _dot(merged.astype(BF16), wo_ref[...])


def _merge(x2, oa, o_f, o_b, p2, norm_g, wa, wb, wo, tm=512):
    n = x2.shape[0]
    const = lambda i: (0, 0)
    return pl.pallas_call(
        _merge_kernel,
        grid=(n // tm,),
        in_specs=[pl.BlockSpec((tm, D), lambda i: (i, 0)),
                  pl.BlockSpec((tm, HW), lambda i: (i, 0)),
                  pl.BlockSpec((tm, HW), lambda i: (i, 0)),
                  pl.BlockSpec((tm, HW), lambda i: (i, 0)),
                  pl.BlockSpec((tm, HW), lambda i: (i, C_Z // HW)),
                  pl.BlockSpec((tm, D), lambda i: (i, C_MA // D)),
                  pl.BlockSpec((tm, D), lambda i: (i, C_MB // D)),
                  pl.BlockSpec((1, HD), const),
                  pl.BlockSpec((HW, D), const),
                  pl.BlockSpec((HW, D), const),
                  pl.BlockSpec((D, D), const)],
        out_specs=pl.BlockSpec((tm, D), lambda i: (i, 0)),
        out_shape=jax.ShapeDtypeStruct((n, D), F32),
        compiler_params=_params(("parallel",)),
        name="merge",
    )(x2, oa, o_f, o_b, p2, p2, p2, norm_g.reshape(1, HD), wa, wb, wo)


def _top_rows(s, payload=None):
    r = s.shape[0]
    row = lax.broadcasted_iota(jnp.int32, s.shape, 0).astype(F32)
    vals, idxs = [], []
    for _ in range(P_TOPK):
        m = jnp.max(s, axis=0, keepdims=True)
        first = jnp.min(jnp.where(s == m, row, float(r)), axis=0, keepdims=True)
        sel = row == first
        vals.append(m)
        idxs.append(first if payload is None else jnp.max(jnp.where(sel, payload, -1.0), axis=0, keepdims=True))
        s = jnp.where(sel, -jnp.inf, s)
    return vals, idxs


def _stack_rows(rows):
    t = rows[0].shape[1]
    row = lax.broadcasted_iota(jnp.int32, (len(rows), t), 0)
    out = jnp.zeros((len(rows), t), F32)
    for i, r in enumerate(rows):
        out = jnp.where(row == i, r, out)
    return out


def _pair_candidates(v1, i1, v2, i2):
    v1s, i1s, v2s, i2s = _stack_rows(v1), _stack_rows(i1), _stack_rows(v2), _stack_rows(i2)
    brow = lax.broadcasted_iota(jnp.int32, (8, v2s.shape[1]), 0)
    cand = [v1[0] + v2s]
    cidx = [i1[0] * P_NKEYS + i2s]
    for a in range(1, 8):
        ok = brow < P_TOPK // (a + 1)
        cand.append(jnp.where(ok, v1[a] + v2s[0:8], -jnp.inf))
        cidx.append(i1[a] * P_NKEYS + i2s[0:8])
    cand.append(v1s[8:] + v2[0])
    cidx.append(i1s[8:] * P_NKEYS + i2[0])
    return jnp.concatenate(cand, axis=0), jnp.concatenate(cidx, axis=0)


def _route_kernel(x1_ref, g_ref, wq_ref, k1_ref, k2_ref, h2_ref, eidx_ref, gate_ref):
    x = x1_ref[...]
    h2 = x * lax.rsqrt(jnp.mean(x * x, axis=-1, keepdims=True) + EPS) * g_ref[...]
    h2_ref[...] = h2
    qp = _dot(h2.astype(BF16), wq_ref[...])
    gates, eidxs = [], []
    for h in range(P_HEADS):
        q1 = qp[:, h * P_DK:h * P_DK + P_DK // 2].astype(BF16)
        q2 = qp[:, h * P_DK + P_DK // 2:(h + 1) * P_DK].astype(BF16)
        s1 = _dot_nt(k1_ref[h], q1)
        s2 = _dot_nt(k2_ref[h], q2)
        v1, i1 = _top_rows(s1)
        v2, i2 = _top_rows(s2)
        cand, cidx = _pair_candidates(v1, i1, v2, i2)
        sc, ei = _top_rows(cand, cidx)
        ex = jnp.exp(_stack_rows(sc) - sc[0])
        gates.append(ex / jnp.sum(ex, axis=0, keepdims=True))
        eidxs.append(_stack_rows(ei) * float(SUB))
    gate_ref[...] = jnp.concatenate(gates, axis=0).T
    c = lax.broadcasted_iota(jnp.int32, (P_NSEL, P_NSEL), 0)
    s = lax.broadcasted_iota(jnp.int32, (P_NSEL, P_NSEL), 1)
    perm = jnp.where(s == OFF_WAYS * (c % OFF_PER) + c // OFF_PER, 1.0, 0.0)
    dealt = _dot(perm, jnp.concatenate(eidxs, axis=0), precision=HIGHEST)
    eidx_ref[...] = dealt.T.astype(jnp.int32)


def _route(x1, g_ffn, wq, k1, k2, tm=128):
    n = x1.shape[0]
    return pl.pallas_call(
        _route_kernel,
        grid=(n // tm,),
        in_specs=[pl.BlockSpec((tm, D), lambda i: (i, 0)),
                  pl.BlockSpec((1, D), lambda i: (0, 0)),
                  pl.BlockSpec((D, P_HEADS * P_DK), lambda i: (0, 0)),
                  pl.BlockSpec((P_HEADS, P_NKEYS, P_DK // 2), lambda i: (0, 0, 0)),
                  pl.BlockSpec((P_HEADS, P_NKEYS, P_DK // 2), lambda i: (0, 0, 0))],
        out_specs=[pl.BlockSpec((tm, D), lambda i: (i, 0)),
                   pl.BlockSpec((tm, P_NSEL), lambda i: (i, 0)),
                   pl.BlockSpec((tm, P_NSEL), lambda i: (i, 0))],
        out_shape=[jax.ShapeDtypeStruct((n, D), F32),
                   jax.ShapeDtypeStruct((n, P_NSEL), jnp.int32),
                   jax.ShapeDtypeStruct((n, P_NSEL), F32)],
        compiler_params=_params(("parallel",)),
        name="route",
    )(x1, g_ffn.reshape(1, D), wq, k1, k2)


def _pack_table(t):
    bits = lax.bitcast_convert_type(t.astype(BF16), jnp.uint16).astype(jnp.uint32)
    packed = bits[:, :D // 2] | (bits[:, D // 2:] << 16)
    return lax.bitcast_convert_type(packed, jnp.int32).reshape(t.shape[0] * SUB, LANES)


PEER_UNROLL = 16
NROWS = P_NSEL * SUB
NCOL = 2 * NROWS


OFF_WAYS = 8
OFF_PER = P_NSEL // OFF_WAYS


def _deal_offsets(eoff):
    n = eoff.shape[0]
    return [eoff[:, OFF_PER * j:OFF_PER * (j + 1)].reshape(n * OFF_PER) for j in range(OFF_WAYS)]


def _gather_rows(off_refs, tab_ref, g_ref, t):
    off = 0
    for i in range(OFF_PER):
        pos = t * OFF_PER + i
        for j, off_ref in enumerate(off_refs):
            k = OFF_WAYS * i + j
            off = pl.multiple_of(off_ref[pos], SUB)
            g_ref[SUB * k:SUB * (k + 1), :] = tab_ref[pl.ds(off, SUB), :]
    return off


def _two_buffer_loop(tm, gather, compute, ga_ref, gb_ref):
    gather(0, ga_ref)
    bufs = (ga_ref, gb_ref)

    def body(i, carry):
        t = PEER_UNROLL * i
        last = 0
        for j in range(PEER_UNROLL):
            compute(t + j, bufs[j % 2])
            nxt = jnp.minimum(t + j + 1, tm - 1) + jnp.minimum(last, 0)
            last = gather(nxt, bufs[(j + 1) % 2])
        return carry

    lax.fori_loop(0, tm // PEER_UNROLL, body, 0)


def _diag_mask():
    j = lax.broadcasted_iota(jnp.int32, (2 * SUB, NCOL), 0)
    c = lax.broadcasted_iota(jnp.int32, (2 * SUB, NCOL), 1)
    return (c & 7) == ((j & 3) * 2 + (j >> 2))


def _split_bf16(a):
    hi = a.astype(BF16)
    lo = (a - hi.astype(F32)).astype(BF16)
    return jnp.concatenate([hi, lo], axis=0)


def _peer_u_kernel(*refs, tm):
    off_refs = refs[:OFF_WAYS]
    x_ref, gate_ref, tab_ref, sel_ref, w_ref, ga_ref, gb_ref, r_ref = refs[OFF_WAYS:]
    mask = _diag_mask()

    def gather(t, g_ref):
        return _gather_rows(off_refs, tab_ref, g_ref, t)

    def compute(t, g_ref):
        g16 = pltpu.bitcast(g_ref[...], BF16)
        x8 = x_ref[pl.ds(t, 1), :].reshape(2 * SUB, LANES)
        o = _dot_nt(_split_bf16(x8), g16)
        o8 = o[0:2 * SUB] + o[2 * SUB:]
        r_ref[pl.ds(t, 1), :] = jnp.sum(jnp.where(mask, o8, 0.0), axis=0, keepdims=True)

    _two_buffer_loop(tm, gather, compute, ga_ref, gb_ref)
    pre = _dot(r_ref[...], sel_ref[...], precision=HIGHEST)
    act = 0.5 * pre * (1.0 + lax.erf(pre * (2.0 ** -0.5)))
    w_ref[...] = gate_ref[...] * act


def _group_matrix():
    c = jnp.arange(NCOL, dtype=jnp.int32)[:, None] // 8
    k = jnp.arange(P_NSEL, dtype=jnp.int32)[None, :]
    return (c == k).astype(F32)


def _off_specs(tm):
    return [pl.BlockSpec((tm * OFF_PER,), lambda i: (i,), memory_space=pltpu.SMEM) for _ in range(OFF_WAYS)]


def _peer_u(eoff, h2r, gate, tab, tm=256):
    n = eoff.shape[0]
    return pl.pallas_call(
        functools.partial(_peer_u_kernel, tm=tm),
        grid=(n // tm,),
        in_specs=_off_specs(tm) + [
                  pl.BlockSpec((tm, D), lambda i: (i, 0)),
                  pl.BlockSpec((tm, P_NSEL), lambda i: (i, 0)),
                  pl.BlockSpec(memory_space=pltpu.VMEM),
                  pl.BlockSpec((NCOL, P_NSEL), lambda i: (0, 0))],
        out_specs=pl.BlockSpec((tm, P_NSEL), lambda i: (i, 0)),
        out_shape=jax.ShapeDtypeStruct((n, P_NSEL), F32),
        scratch_shapes=[pltpu.VMEM((NROWS, LANES), jnp.int32), pltpu.VMEM((NROWS, LANES), jnp.int32),
                        pltpu.VMEM((tm, NCOL), F32)],
        compiler_params=_params(("arbitrary",)),
        name="peer_u",
    )(*_deal_offsets(eoff), h2r, gate, tab, _group_matrix())


def _peer_v_kernel(*refs, tm):
    off_refs = refs[:OFF_WAYS]
    w_ref, tab_ref, rep_ref, x1_ref, gf_ref, y_ref, ga_ref, gb_ref, w8_ref, po_ref = refs[OFF_WAYS:]
    mask = _diag_mask()
    w8_ref[...] = _dot(w_ref[...], rep_ref[...], precision=HIGHEST)
    row8 = lax.broadcasted_iota(jnp.int32, (2 * SUB, LANES), 0)

    def gather(t, g_ref):
        return _gather_rows(off_refs, tab_ref, g_ref, t)

    def compute(t, g_ref):
        g16 = pltpu.bitcast(g_ref[...], BF16)
        wexp = jnp.where(mask, jnp.broadcast_to(w8_ref[pl.ds(t, 1), :], (2 * SUB, NCOL)), 0.0)
        o = _dot(_split_bf16(wexp), g16)
        o8 = o[0:2 * SUB] + o[2 * SUB:]
        pieces = [jnp.sum(jnp.where(row8 == j, o8, 0.0), axis=0, keepdims=True) for j in range(2 * SUB)]
        po_ref[pl.ds(t, 1), :] = jnp.concatenate(pieces, axis=1)

    _two_buffer_loop(tm, gather, compute, ga_ref, gb_ref)
    x = x1_ref[...] + po_ref[...]
    y_ref[...] = x * lax.rsqrt(jnp.mean(x * x, axis=-1, keepdims=True) + EPS) * gf_ref[...]


def _peer_v(eoff, w, tab, x1, g_final, tm=256):
    n = eoff.shape[0]
    return pl.pallas_call(
        functools.partial(_peer_v_kernel, tm=tm),
        grid=(n // tm,),
        in_specs=_off_specs(tm) + [
                  pl.BlockSpec((tm, P_NSEL), lambda i: (i, 0)),
                  pl.BlockSpec(memory_space=pltpu.VMEM),
                  pl.BlockSpec((P_NSEL, NCOL), lambda i: (0, 0)),
                  pl.BlockSpec((tm, D), lambda i: (i, 0)),
                  pl.BlockSpec((1, D), lambda i: (0, 0))],
        out_specs=pl.BlockSpec((tm, D), lambda i: (i, 0)),
        out_shape=jax.ShapeDtypeStruct((n, D), F32),
        scratch_shapes=[pltpu.VMEM((NROWS, LANES), jnp.int32), pltpu.VMEM((NROWS, LANES), jnp.int32),
                        pltpu.VMEM((tm, NCOL), F32), pltpu.VMEM((tm, D), F32)],
        compiler_params=_params(("arbitrary",)),
        name="peer_v",
    )(*_deal_offsets(eoff), w, tab, _group_matrix().T, x1, g_final.reshape(1, D))


def _pack_w_in(w_in):
    w = jnp.concatenate([w_in[:, 2048:3584], w_in[:, :2048], w_in[:, 3584:4096], w_in[:, 4112:6160],
                         w_in[:, 4096:4112], jnp.zeros((D, PW - C_SMALL - 16), w_in.dtype)], axis=1)
    return w.astype(BF16)


def _encoder(x, wts):
    b, s, _ = x.shape
    n = b * s
    x2 = x.reshape(n, D)
    p2 = _in_proj(x2, wts["g_mix"], wts["w_cat"])
    p3 = p2.reshape(b, s, PW)
    oa = _retention(p3, wts["ret_gn"]).reshape(n, HW)
    qkv, sm = _gdn_prep(p3, wts["dn_conv"], wts["dn_a_log"], wts["dn_dt_bias"])
    o_f, o_b = _gdn_scan(_gdn_terms(qkv, sm), nb=4 if b % 4 == 0 else 1)
    x1 = _merge(x2, oa, o_f.reshape(n, HW), o_b.reshape(n, HW), p2, wts["dn_norm"],
                wts["wa"], wts["wb"], wts["wo"])
    h2, eidx, gate = _route(x1, wts["g_ffn"], wts["wq"], wts["k1"], wts["k2"])
    w = _peer_u(eidx, h2, gate, wts["tab_u"])
    return _peer_v(eidx, w, wts["tab_v"], x1, wts["g_final"]).reshape(b, s, D)


def kernel(x_prompt, x_sample, g_mix, w_in, ret_gn, dn_conv, dn_a_log, dn_dt_bias, dn_norm, w_branch_a, w_branch_b, w_out, g_ffn, peer_wq, peer_k1, peer_k2, peer_u, peer_v, g_final):
    wts = dict(
        g_mix=g_mix[0], w_cat=_pack_w_in(w_in[0]), ret_gn=ret_gn[0], dn_conv=dn_conv[0],
        dn_a_log=dn_a_log[0], dn_dt_bias=dn_dt_bias[0], dn_norm=dn_norm[0],
        wa=w_branch_a[0].astype(BF16), wb=w_branch_b[0].astype(BF16), wo=w_out[0].astype(BF16),
        g_ffn=g_ffn[0], wq=peer_wq[0].astype(BF16), k1=peer_k1[0].astype(BF16), k2=peer_k2[0].astype(BF16),
        tab_u=_pack_table(peer_u[0]), tab_v=_pack_table(peer_v[0]), g_final=g_final)
    return (_encoder(x_prompt, wts), _encoder(x_sample, wts))
```

```python
import functools

import jax
import jax.numpy as jnp
import numpy as np
from jax import lax
from jax.experimental import pallas as pl
from jax.experimental.pallas import tpu as pltpu

F32 = jnp.float32
BF16 = jnp.bfloat16
HIGHEST = lax.Precision.HIGHEST

D = 1024
EPS = 1e-6
NH = 4
HD = 128
HW = NH * HD
ROPE_BASE = 10000.0
DN_CONV = 5
DN_C = 64
RET_C = 512
P_HEADS = 8
P_DK = 256
P_NKEYS = 128
P_TOPK = 16
P_NSEL = P_HEADS * P_TOPK
LANES = 128
SUB = 4
VMEM_LIMIT = 56 * 1024 * 1024

C_DNQKV = 0
C_RET = 1536
C_Z = 3584
C_MA = 4096
C_MB = 5120
C_SMALL = 6144
PW = 6272
PCH = 896


def _dot(a, b, **kw):
    return jnp.dot(a, b, preferred_element_type=F32, **kw)


def _dot_nt(a, b, **kw):
    return lax.dot_general(a, b, (((1,), (1,)), ((), ())), preferred_element_type=F32, **kw)


def _params(sem=None):
    return pltpu.CompilerParams(dimension_semantics=sem, vmem_limit_bytes=VMEM_LIMIT)


def _in_proj_kernel(x_ref, g_ref, w_ref, p_ref):
    x = x_ref[...]
    h = x * lax.rsqrt(jnp.mean(x * x, axis=-1, keepdims=True) + EPS) * g_ref[...]
    h = h.astype(BF16)
    for c in range(0, PW, PCH):
        p_ref[:, c:c + PCH] = _dot(h, w_ref[:, c:c + PCH])


def _in_proj(x2, g, w_cat, tm=256):
    n = x2.shape[0]
    return pl.pallas_call(
        _in_proj_kernel,
        grid=(n // tm,),
        in_specs=[pl.BlockSpec((tm, D), lambda i: (i, 0)),
                  pl.BlockSpec((1, D), lambda i: (0, 0)),
                  pl.BlockSpec((D, PW), lambda i: (0, 0))],
        out_specs=pl.BlockSpec((tm, PW), lambda i: (i, 0)),
        out_shape=jax.ShapeDtypeStruct((n, PW), F32),
        compiler_params=_params(("parallel",)),
        name="in_proj",
    )(x2, g.reshape(1, D), w_cat)


def _ret_kernel(q_ref, k_ref, v_ref, gt_ref, cos_ref, sin_ref, dmat_ref, vec_ref, cdec_ref, gn_ref,
                o_ref, run_ref, stash_ref):
    ph = pl.program_id(1)
    c = pl.program_id(2)
    nc = pl.num_programs(2)

    @pl.when(c == 0)
    def _():
        run_ref[...] = jnp.zeros_like(run_ref)

    cos2 = cos_ref[...]
    sinm = sin_ref[...]

    def rot(t):
        return t * cos2 + pltpu.roll(t, HD // 2, axis=1) * sinm

    def state_update(h, kh, vh, zeta, dec_row):
        kz = (kh * zeta).T.astype(BF16)
        contrib = _dot(kz, vh.astype(BF16))
        run_ref[h] = run_ref[h] * cdec_ref[dec_row:dec_row + 1, :] + contrib

    @pl.when(ph == 0)
    def _():
        ci = nc - 1 - c
        for h in range(NH):
            sl = slice(h * HD, (h + 1) * HD)
            kh = rot(k_ref[0, :, sl])
            stash_ref[ci, h] = run_ref[h]
            state_update(h, kh, v_ref[0, :, sl], vec_ref[3, h], NH + h)

    @pl.when(ph == 1)
    def _():
        for h in range(NH):
            sl = slice(h * HD, (h + 1) * HD)
            qh = rot(q_ref[0, :, sl]) * (HD ** -0.5)
            kh = rot(k_ref[0, :, sl])
            vh = v_ref[0, :, sl]
            vb = vh.astype(BF16)
            scores = _dot_nt(qh.astype(BF16), kh.astype(BF16)) * dmat_ref[h]
            o = _dot(scores.astype(BF16), vb)
            o += _dot((qh * vec_ref[0, h]).astype(BF16), run_ref[h].astype(BF16))
            o += _dot((qh * vec_ref[1, h]).astype(BF16), stash_ref[c, h].astype(BF16))
            state_update(h, kh, vh, vec_ref[2, h], h)
            mu = jnp.mean(o, axis=-1, keepdims=True)
            oc = o - mu
            var = jnp.mean(oc * oc, axis=-1, keepdims=True)
            on = oc * lax.rsqrt(var + EPS) * gn_ref[:, sl]
            gate = gt_ref[0, :, sl]
            o_ref[0, :, sl] = (on * (gate * jax.nn.sigmoid(gate))).astype(o_ref.dtype)


def _ret_tables(s):
    half = HD // 2
    inv_freq = 1.0 / (ROPE_BASE ** jnp.linspace(0.0, 1.0, half, dtype=F32))
    ang = jnp.arange(s, dtype=F32)[:, None] * inv_freq[None, :]
    cos, sin = jnp.cos(ang), jnp.sin(ang)
    cos2 = jnp.concatenate([cos, cos], axis=-1)
    sinm = jnp.concatenate([-sin, sin], axis=-1)
    lg = jnp.log(1.0 - 2.0 ** (-5.0 - jnp.arange(NH, dtype=F32)))
    lgb = lg[::-1]
    c = RET_C
    pos = jnp.arange(c, dtype=F32)
    diff = pos[:, None] - pos[None, :]
    dmat = jnp.where(diff >= 0, jnp.exp(lg[:, None, None] * jnp.maximum(diff, 0.0)),
                     jnp.exp(lgb[:, None, None] * jnp.maximum(-diff, 0.0)))
    xi_f = jnp.exp(lg[:, None] * (pos + 1.0))
    xi_b = jnp.exp(lgb[:, None] * (c - pos))
    zeta_f = jnp.exp(lg[:, None] * (c - 1 - pos))
    zeta_b = jnp.exp(lgb[:, None] * pos)
    vec = jnp.broadcast_to(jnp.stack([xi_f, xi_b, zeta_f, zeta_b])[..., None], (4, NH, c, LANES))
    cdec = jnp.concatenate([jnp.exp(lg * c), jnp.exp(lgb * c)])
    cdec = jnp.broadcast_to(cdec[:, None], (2 * NH, LANES))
    return cos2, sinm, dmat, vec, cdec


def _retention(p3, ret_gn):
    b, s, _ = p3.shape
    c = RET_C
    nc = s // c
    cos2, sinm, dmat, vec, cdec = _ret_tables(s)

    def chunk(ph, ci):
        return jnp.where(ph == 0, nc - 1 - ci, ci)

    def pspec(col):
        return pl.BlockSpec((1, c, HW), lambda bi, ph, ci: (bi, chunk(ph, ci), col))

    base = C_RET // HW
    return pl.pallas_call(
        _ret_kernel,
        grid=(b, 2, nc),
        in_specs=[pspec(base), pspec(base + 1), pspec(base + 2), pspec(base + 3),
                  pl.BlockSpec((c, HD), lambda bi, ph, ci: (chunk(ph, ci), 0)),
                  pl.BlockSpec((c, HD), lambda bi, ph, ci: (chunk(ph, ci), 0)),
                  pl.BlockSpec((NH, c, c), lambda bi, ph, ci: (0, 0, 0)),
                  pl.BlockSpec((4, NH, c, LANES), lambda bi, ph, ci: (0, 0, 0, 0)),
                  pl.BlockSpec((2 * NH, LANES), lambda bi, ph, ci: (0, 0)),
                  pl.BlockSpec((1, HW), lambda bi, ph, ci: (0, 0))],
        out_specs=pl.BlockSpec((1, c, HW), lambda bi, ph, ci: (bi, ci * ph, 0)),
        out_shape=jax.ShapeDtypeStruct((b, s, HW), BF16),
        scratch_shapes=[pltpu.VMEM((NH, HD, HD), F32), pltpu.VMEM((nc, NH, HD, HD), F32)],
        compiler_params=_params(("parallel", "arbitrary", "arbitrary")),
        name="retention",
    )(p3, p3, p3, p3, cos2, sinm, dmat, vec, cdec, ret_gn.reshape(1, HW))


def _gdn_prep_kernel(prev_ref, cur_ref, next_ref, sm_ref, cw_ref, alog_ref, dtb_ref, qkv_ref, smo_ref, *, tc):
    i = pl.program_id(1)
    last = pl.num_programs(1) - 1
    prev = jnp.where(i == 0, 0.0, prev_ref[0])
    nxt = jnp.where(i == last, 0.0, next_ref[0])
    ext = jnp.concatenate([prev, cur_ref[0], nxt], axis=0)
    nrow = tc + 16
    half = DN_CONV // 2
    y = jnp.zeros((tc, 3 * HW), F32)
    for j in range(DN_CONV):
        shifted = ext if j == half else pltpu.roll(ext, (half - j) % nrow, axis=0)
        y = y + shifted[8:8 + tc] * cw_ref[j:j + 1, :]
    y = y * jax.nn.sigmoid(y)
    for h in range(2 * NH):
        sl = slice(h * HD, (h + 1) * HD)
        t = y[:, sl]
        t = t * lax.rsqrt(jnp.sum(t * t, axis=-1, keepdims=True) + EPS)
        if h < NH:
            t = t * (HD ** -0.5)
        qkv_ref[0, :, sl] = t
    qkv_ref[0, :, 2 * HW:] = y[:, 2 * HW:]
    a = sm_ref[0]
    lane = lax.broadcasted_iota(jnp.int32, a.shape, 1)
    z = a + dtb_ref[...]
    softplus = jnp.maximum(z, 0.0) + jnp.log1p(jnp.exp(-jnp.abs(z)))
    g = -jnp.exp(alog_ref[...]) * softplus
    smo_ref[0] = jnp.where(lane < 2 * NH, jax.nn.sigmoid(a), g)


def _gdn_prep(p3, conv_w, a_log, dt_bias, tc=512):
    b, s, _ = p3.shape
    nt = s // tc
    r8 = tc // 8
    w3 = 3 * HW
    zeros8 = jnp.zeros((2 * NH,), F32)
    pad = jnp.zeros((LANES - 4 * NH,), F32)
    alog = jnp.concatenate([zeros8, a_log[0], a_log[1], pad]).reshape(1, LANES)
    dtb = jnp.concatenate([zeros8, dt_bias[0], dt_bias[1], pad]).reshape(1, LANES)
    return pl.pallas_call(
        functools.partial(_gdn_prep_kernel, tc=tc),
        grid=(b, nt),
        in_specs=[pl.BlockSpec((1, 8, w3), lambda bi, i: (bi, jnp.maximum(i * r8 - 1, 0), 0)),
                  pl.BlockSpec((1, tc, w3), lambda bi, i: (bi, i, 0)),
                  pl.BlockSpec((1, 8, w3), lambda bi, i: (bi, jnp.minimum((i + 1) * r8, s // 8 - 1), 0)),
                  pl.BlockSpec((1, tc, LANES), lambda bi, i: (bi, i, C_SMALL // LANES)),
                  pl.BlockSpec((DN_CONV, w3), lambda bi, i: (0, 0)),
                  pl.BlockSpec((1, LANES), lambda bi, i: (0, 0)),
                  pl.BlockSpec((1, LANES), lambda bi, i: (0, 0))],
        out_specs=[pl.BlockSpec((1, tc, w3), lambda bi, i: (bi, i, 0)),
                   pl.BlockSpec((1, tc, LANES), lambda bi, i: (bi, i, 0))],
        out_shape=[jax.ShapeDtypeStruct((b, s, w3), F32), jax.ShapeDtypeStruct((b, s, LANES), F32)],
        compiler_params=_params(("parallel", "parallel")),
        name="gdn_prep",
    )(p3, p3, p3, p3, conv_w, alog, dtb)


def _gdn_terms_kernel(qkv_ref, sm_ref, *out_refs, nchunk):
    c = DN_C
    row = lax.broadcasted_iota(jnp.int32, (c, c), 0)
    col = lax.broadcasted_iota(jnp.int32, (c, c), 1)
    lane = lax.broadcasted_iota(jnp.int32, (c, LANES), 1)
    gt_ref = out_refs[-1]
    ids, qn, kn, vv, gcum, beta, lmat, strict = [], [], [], [], [], [], [], []
    for ci in range(nchunk):
        rows = slice(ci * c, (ci + 1) * c)
        sm = sm_ref[0, rows, :]
        for d in range(2):
            tri = jnp.where((row <= col) if d else (row >= col), 1.0, 0.0)
            gc_all = _dot(tri, sm, precision=HIGHEST)
            gc_t = gc_all.T
            incl = (row <= col) if d else (row >= col)
            for h in range(NH):
                lg = 2 * NH + NH * d + h
                ids.append((ci, d, h))
                qn.append(qkv_ref[0, rows, h * HD:(h + 1) * HD])
                kn.append(qkv_ref[0, rows, HW + h * HD:HW + (h + 1) * HD])
                vv.append(qkv_ref[0, rows, 2 * HW + h * HD:2 * HW + (h + 1) * HD])
                beta.append(jnp.sum(jnp.where(lane == NH * d + h, sm, 0.0), axis=-1, keepdims=True))
                g = jnp.sum(jnp.where(lane == lg, gc_all, 0.0), axis=-1, keepdims=True)
                gcum.append(g)
                lmat.append(jnp.where(incl, jnp.exp(jnp.where(incl, g - gc_t[lg:lg + 1, :], 0.0)), 0.0))
                strict.append((row < col) if d else (row > col))
    n = len(ids)
    rng = range(n)
    kb = [kn[i] * beta[i] for i in rng]
    knb = [kn[i].astype(BF16) for i in rng]
    kk = [_dot_nt(kb[i].astype(BF16), knb[i]) for i in rng]
    a = [jnp.where(strict[i], kk[i] * lmat[i], 0.0) for i in rng]
    ab = [a[i].astype(BF16) for i in rng]
    p = [_dot(ab[i], ab[i]) for i in rng]
    m = [-a[i] for i in rng]
    steps = int(np.log2(c)) - 1
    for it in range(steps):
        pb = [p[i].astype(BF16) for i in rng]
        mp = [_dot(m[i].astype(BF16), pb[i]) for i in rng]
        m = [m[i] + p[i] + mp[i] for i in rng]
        if it + 1 < steps:
            p = [_dot(pb[i], pb[i]) for i in rng]
    eg = [jnp.exp(gcum[i]) for i in rng]
    rhs = [jnp.concatenate([vv[i] * beta[i], kb[i] * eg[i]], axis=-1) for i in rng]
    corr = [_dot(m[i].astype(BF16), rhs[i].astype(BF16)) for i in rng]
    attn = [_dot_nt(qn[i].astype(BF16), knb[i]) for i in rng]
    for i, (ci, d, h) in enumerate(ids):
        u_ref, w_ref, qd_ref, kdt_ref, att_ref = out_refs[5 * d:5 * d + 5]
        rows = slice(ci * c, (ci + 1) * c)
        sl = slice(h * HD, (h + 1) * HD)
        sol = rhs[i] + corr[i]
        gl = gcum[i][0:1, :] if d else gcum[i][c - 1:c, :]
        u_ref[0, rows, sl] = sol[:, :HD]
        w_ref[0, rows, sl] = sol[:, HD:].astype(BF16)
        qd_ref[0, rows, sl] = (qn[i] * eg[i]).astype(BF16)
        kdt_ref[0, ci, h] = (kn[i] * jnp.exp(gl - gcum[i])).T.astype(BF16)
        att_ref[0, ci, h] = (attn[i] * lmat[i]).astype(BF16)
        gt_ref[0, ci, NH * d + h:NH * d + h + 1, :] = jnp.broadcast_to(jnp.exp(gl), (1, LANES))


def _gdn_terms(qkv, sm, nchunk=4):
    b, s, w3 = qkv.shape
    c = DN_C
    nc = s // c
    r = nchunk * c
    tok = lambda dt: (pl.BlockSpec((1, r, HW), lambda bi, i: (bi, i, 0)), jax.ShapeDtypeStruct((b, s, HW), dt))
    kdt = (pl.BlockSpec((1, nchunk, NH, HD, c), lambda bi, i: (bi, i, 0, 0, 0)),
           jax.ShapeDtypeStruct((b, nc, NH, HD, c), BF16))
    att = (pl.BlockSpec((1, nchunk, NH, c, c), lambda bi, i: (bi, i, 0, 0, 0)),
           jax.ShapeDtypeStruct((b, nc, NH, c, c), BF16))
    gt = (pl.BlockSpec((1, nchunk, 2 * NH, LANES), lambda bi, i: (bi, i, 0, 0)),
          jax.ShapeDtypeStruct((b, nc, 2 * NH, LANES), F32))
    outs = [tok(F32), tok(BF16), tok(BF16), kdt, att] * 2 + [gt]
    return pl.pallas_call(
        functools.partial(_gdn_terms_kernel, nchunk=nchunk),
        grid=(b, nc // nchunk),
        in_specs=[pl.BlockSpec((1, r, w3), lambda bi, i: (bi, i, 0)),
                  pl.BlockSpec((1, r, LANES), lambda bi, i: (bi, i, 0))],
        out_specs=[o[0] for o in outs],
        out_shape=[o[1] for o in outs],
        compiler_params=_params(("parallel", "parallel")),
        name="gdn_terms",
    )(qkv, sm)


def _gdn_scan_kernel(*refs, nb):
    (uf, wf, qdf, kdtf, attf, gtf, ub, wb, qdb, kdtb, attb, gtb, of_ref, ob_ref, s_ref) = refs

    @pl.when(pl.program_id(1) == 0)
    def _():
        s_ref[...] = jnp.zeros_like(s_ref)

    dirs = ((uf, wf, qdf, kdtf, attf, gtf, of_ref), (ub, wb, qdb, kdtb, attb, gtb, ob_ref))
    ids = [(bi, d, h) for bi in range(nb) for d in range(2) for h in range(NH)]
    sl = [slice(h * HD, (h + 1) * HD) for h in range(NH)]
    s = [s_ref[bi, d, h] for bi, d, h in ids]
    sb = [t.astype(BF16) for t in s]
    ws = [_dot(dirs[d][1][bi, :, sl[h]], sb[i]) for i, (bi, d, h) in enumerate(ids)]
    qs = [_dot(dirs[d][2][bi, :, sl[h]], sb[i]) for i, (bi, d, h) in enumerate(ids)]
    vnb = [(dirs[d][0][bi, :, sl[h]] - ws[i]).astype(BF16) for i, (bi, d, h) in enumerate(ids)]
    av = [_dot(dirs[d][4][bi, 0, h], vnb[i]) for i, (bi, d, h) in enumerate(ids)]
    kv = [_dot(dirs[d][3][bi, 0, h], vnb[i]) for i, (bi, d, h) in enumerate(ids)]
    for i, (bi, d, h) in enumerate(ids):
        dirs[d][6][bi, :, sl[h]] = qs[i] + av[i]
        gt = dirs[d][5][bi, 0, NH * d + h:NH * d + h + 1, :]
        s_ref[bi, d, h] = s[i] * gt + kv[i]


def _gdn_scan(terms, nb):
    uf, wf, qdf, kdtf, attf, ub, wb, qdb, kdtb, attb, gt = terms
    b, s, _ = uf.shape
    c = DN_C
    nc = s // c
    fwd = lambda bi, i: i
    bwd = lambda bi, i: nc - 1 - i

    def specs(pos):
        return [pl.BlockSpec((nb, c, HW), lambda bi, i: (bi, pos(bi, i), 0)),
                pl.BlockSpec((nb, c, HW), lambda bi, i: (bi, pos(bi, i), 0)),
                pl.BlockSpec((nb, c, HW), lambda bi, i: (bi, pos(bi, i), 0)),
                pl.BlockSpec((nb, 1, NH, HD, c), lambda bi, i: (bi, pos(bi, i), 0, 0, 0)),
                pl.BlockSpec((nb, 1, NH, c, c), lambda bi, i: (bi, pos(bi, i), 0, 0, 0)),
                pl.BlockSpec((nb, 1, 2 * NH, LANES), lambda bi, i: (bi, pos(bi, i), 0, 0))]

    return pl.pallas_call(
        functools.partial(_gdn_scan_kernel, nb=nb),
        grid=(b // nb, nc),
        in_specs=specs(fwd) + specs(bwd),
        out_specs=[pl.BlockSpec((nb, c, HW), lambda bi, i: (bi, i, 0)),
                   pl.BlockSpec((nb, c, HW), lambda bi, i: (bi, nc - 1 - i, 0))],
        out_shape=[jax.ShapeDtypeStruct((b, s, HW), F32), jax.ShapeDtypeStruct((b, s, HW), F32)],
        scratch_shapes=[pltpu.VMEM((nb, 2, NH, HD, HD), F32)],
        compiler_params=_params(("parallel", "arbitrary")),
        name="gdn_scan",
    )(uf, wf, qdf, kdtf, attf, gt, ub, wb, qdb, kdtb, attb, gt)


def _merge_kernel(x_ref, oa_ref, of_ref, ob_ref, z_ref, ma_ref, mb_ref, ng_ref, wa_ref, wb_ref, wo_ref, x1_ref):
    ob_parts = []
    for h in range(NH):
        sl = slice(h * HD, (h + 1) * HD)
        o = of_ref[:, sl] + ob_ref[:, sl]
        o = o * lax.rsqrt(jnp.mean(o * o, axis=-1, keepdims=True) + EPS) * ng_ref[...]
        z = z_ref[:, sl]
        ob_parts.append((o * (z * jax.nn.sigmoid(z))).astype(BF16))
    ob = jnp.concatenate(ob_parts, axis=-1)
    ya = _dot(oa_ref[...], wa_ref[...])
    yb = _dot(ob, wb_ref[...])
    merged = jax.nn.sigmoid(ma_ref[...]) * ya + jax.nn.sigmoid(mb_ref[...]) * yb
    x1_ref[...] = x_ref[...] + _dot(merged.astype(BF16), wo_ref[...])


def _merge(x2, oa, o_f, o_b, p2, norm_g, wa, wb, wo, tm=512):
    n = x2.shape[0]
    const = lambda i: (0, 0)
    return pl.pallas_call(
        _merge_kernel,
        grid=(n // tm,),
        in_specs=[pl.BlockSpec((tm, D), lambda i: (i, 0)),
                  pl.BlockSpec((tm, HW), lambda i: (i, 0)),
                  pl.BlockSpec((tm, HW), lambda i: (i, 0)),
                  pl.BlockSpec((tm, HW), lambda i: (i, 0)),
                  pl.BlockSpec((tm, HW), lambda i: (i, C_Z // HW)),
                  pl.BlockSpec((tm, D), lambda i: (i, C_MA // D)),
                  pl.BlockSpec((tm, D), lambda i: (i, C_MB // D)),
                  pl.BlockSpec((1, HD), const),
                  pl.BlockSpec((HW, D), const),
                  pl.BlockSpec((HW, D), const),
                  pl.BlockSpec((D, D), const)],
        out_specs=pl.BlockSpec((tm, D), lambda i: (i, 0)),
        out_shape=jax.ShapeDtypeStruct((n, D), F32),
        compiler_params=_params(("parallel",)),
        name="merge",
    )(x2, oa, o_f, o_b, p2, p2, p2, norm_g.reshape(1, HD), wa, wb, wo)


def _top_rows(arrays, payload=None):
    r = arrays[0].shape[0]
    row = lax.broadcasted_iota(jnp.int32, arrays[0].shape, 0).astype(F32)
    arrays = list(arrays)
    out = [([], []) for _ in arrays]
    for _ in range(P_TOPK):
        for a, s in enumerate(arrays):
            m = jnp.max(s, axis=0, keepdims=True)
            first = jnp.min(jnp.where(s == m, row, float(r)), axis=0, keepdims=True)
            sel = row == first
            out[a][0].append(m)
            out[a][1].append(first if payload is None
                             else jnp.max(jnp.where(sel, payload, -1.0), axis=0, keepdims=True))
            arrays[a] = jnp.where(sel, -jnp.inf, s)
    return out


def _stack_rows(rows):
    t = rows[0].shape[1]
    row = lax.broadcasted_iota(jnp.int32, (len(rows), t), 0)
    out = jnp.zeros((len(rows), t), F32)
    for i, r in enumerate(rows):
        out = jnp.where(row == i, r, out)
    return out


def _pair_candidates(v1, i1, v2, i2):
    v1s, i1s, v2s, i2s = _stack_rows(v1), _stack_rows(i1), _stack_rows(v2), _stack_rows(i2)
    brow = lax.broadcasted_iota(jnp.int32, (8, v2s.shape[1]), 0)
    cand = [v1[0] + v2s]
    cidx = [i1[0] * P_NKEYS + i2s]
    for a in range(1, 8):
        ok = brow < P_TOPK // (a + 1)
        cand.append(jnp.where(ok, v1[a] + v2s[0:8], -jnp.inf))
        cidx.append(i1[a] * P_NKEYS + i2s[0:8])
    cand.append(v1s[8:] + v2[0])
    cidx.append(i1s[8:] * P_NKEYS + i2[0])
    return jnp.concatenate(cand, axis=0), jnp.concatenate(cidx, axis=0)


def _route_kernel(x1_ref, g_ref, wq_ref, k1_ref, k2_ref, h2_ref, eidx_ref, gate_ref):
    x = x1_ref[...]
    h2 = x * lax.rsqrt(jnp.mean(x * x, axis=-1, keepdims=True) + EPS) * g_ref[...]
    h2_ref[...] = h2
    qp = _dot(h2.astype(BF16), wq_ref[...])
    gates, eidxs = [], []
    for h in range(P_HEADS):
        q1 = qp[:, h * P_DK:h * P_DK + P_DK // 2].astype(BF16)
        q2 = qp[:, h * P_DK + P_DK // 2:(h + 1) * P_DK].astype(BF16)
        s1 = _dot_nt(k1_ref[h], q1)
        s2 = _dot_nt(k2_ref[h], q2)
        (v1, i1), (v2, i2) = _top_rows([s1, s2])
        cand, cidx = _pair_candidates(v1, i1, v2, i2)
        (sc, ei), = _top_rows([cand], cidx)
        ex = jnp.exp(_stack_rows(sc) - sc[0])
        gates.append(ex / jnp.sum(ex, axis=0, keepdims=True))
        eidxs.append(_stack_rows(ei) * float(SUB))
    gate_ref[...] = jnp.concatenate(gates, axis=0).T
    c = lax.broadcasted_iota(jnp.int32, (P_NSEL, P_NSEL), 0)
    s = lax.broadcasted_iota(jnp.int32, (P_NSEL, P_NSEL), 1)
    perm = jnp.where(s == OFF_WAYS * (c % OFF_PER) + c // OFF_PER, 1.0, 0.0)
    dealt = _dot(perm, jnp.concatenate(eidxs, axis=0), precision=HIGHEST)
    eidx_ref[...] = dealt.T.astype(jnp.int32)


def _route(x1, g_ffn, wq, k1, k2, tm=128):
    n = x1.shape[0]
    return pl.pallas_call(
        _route_kernel,
        grid=(n // tm,),
        in_specs=[pl.BlockSpec((tm, D), lambda i: (i, 0)),
                  pl.BlockSpec((1, D), lambda i: (0, 0)),
                  pl.BlockSpec((D, P_HEADS * P_DK), lambda i: (0, 0)),
                  pl.BlockSpec((P_HEADS, P_NKEYS, P_DK // 2), lambda i: (0, 0, 0)),
                  pl.BlockSpec((P_HEADS, P_NKEYS, P_DK // 2), lambda i: (0, 0, 0))],
        out_specs=[pl.BlockSpec((tm, D), lambda i: (i, 0)),
                   pl.BlockSpec((tm, P_NSEL), lambda i: (i, 0)),
                   pl.BlockSpec((tm, P_NSEL), lambda i: (i, 0))],
        out_shape=[jax.ShapeDtypeStruct((n, D), F32),
                   jax.ShapeDtypeStruct((n, P_NSEL), jnp.int32),
                   jax.ShapeDtypeStruct((n, P_NSEL), F32)],
        compiler_params=_params(("parallel",)),
        name="route",
    )(x1, g_ffn.reshape(1, D), wq, k1, k2)


def _pack_table(t):
    bits = lax.bitcast_convert_type(t.astype(BF16), jnp.uint16).astype(jnp.uint32)
    packed = bits[:, :D // 2] | (bits[:, D // 2:] << 16)
    return lax.bitcast_convert_type(packed, jnp.int32).reshape(t.shape[0] * SUB, LANES)


PEER_UNROLL = 32
NROWS = P_NSEL * SUB
NCOL = 2 * NROWS


OFF_WAYS = 8
OFF_PER = P_NSEL // OFF_WAYS


def _deal_offsets(eoff):
    n = eoff.shape[0]
    return [eoff[:, OFF_PER * j:OFF_PER * (j + 1)].reshape(n * OFF_PER) for j in range(OFF_WAYS)]


def _gather_rows(off_refs, tab_ref, g_ref, t):
    off = 0
    for i in range(OFF_PER):
        pos = t * OFF_PER + i
        for j, off_ref in enumerate(off_refs):
            k = OFF_WAYS * i + j
            off = pl.multiple_of(off_ref[pos], SUB)
            g_ref[SUB * k:SUB * (k + 1), :] = tab_ref[pl.ds(off, SUB), :]
    return off


PEER_BUFS = 2


def _buffered_token_loop(tm, gather, compute, bufs):
    lag = len(bufs) - 1
    last = 0
    for j in range(lag):
        last = gather(j + jnp.minimum(last, 0), bufs[j])

    def body(i, carry):
        t = PEER_UNROLL * i
        last = 0
        for j in range(PEER_UNROLL):
            compute(t + j, bufs[j % len(bufs)])
            nxt = jnp.minimum(t + j + lag, tm - 1) + jnp.minimum(last, 0)
            last = gather(nxt, bufs[(j + lag) % len(bufs)])
        return carry

    lax.fori_loop(0, tm // PEER_UNROLL, body, 0)


def _diag_mask():
    j = lax.broadcasted_iota(jnp.int32, (2 * SUB, NCOL), 0)
    c = lax.broadcasted_iota(jnp.int32, (2 * SUB, NCOL), 1)
    return (c & 7) == ((j & 3) * 2 + (j >> 2))


def _split_bf16(a):
    hi = a.astype(BF16)
    lo = (a - hi.astype(F32)).astype(BF16)
    return jnp.concatenate([hi, lo], axis=0)


def _peer_u_kernel(*refs, tm):
    off_refs = refs[:OFF_WAYS]
    x_ref, gate_ref, tab_ref, sel_ref, w_ref = refs[OFF_WAYS:OFF_WAYS + 5]
    bufs, r_ref = refs[OFF_WAYS + 5:-1], refs[-1]
    mask = _diag_mask()

    def gather(t, g_ref):
        return _gather_rows(off_refs, tab_ref, g_ref, t)

    def compute(t, g_ref):
        g16 = pltpu.bitcast(g_ref[...], BF16)
        x8 = x_ref[pl.ds(t, 1), :].reshape(2 * SUB, LANES)
        o = _dot_nt(_split_bf16(x8), g16)
        o8 = o[0:2 * SUB] + o[2 * SUB:]
        r_ref[pl.ds(t, 1), :] = jnp.sum(jnp.where(mask, o8, 0.0), axis=0, keepdims=True)

    _buffered_token_loop(tm, gather, compute, bufs)
    pre = _dot(r_ref[...], sel_ref[...], precision=HIGHEST)
    act = 0.5 * pre * (1.0 + lax.erf(pre * (2.0 ** -0.5)))
    w_ref[...] = gate_ref[...] * act


def _group_matrix():
    c = jnp.arange(NCOL, dtype=jnp.int32)[:, None] // 8
    k = jnp.arange(P_NSEL, dtype=jnp.int32)[None, :]
    return (c == k).astype(F32)


def _off_specs(tm):
    return [pl.BlockSpec((tm * OFF_PER,), lambda i: (i,), memory_space=pltpu.SMEM) for _ in range(OFF_WAYS)]


def _peer_u(eoff, h2r, gate, tab, tm=256):
    n = eoff.shape[0]
    return pl.pallas_call(
        functools.partial(_peer_u_kernel, tm=tm),
        grid=(n // tm,),
        in_specs=_off_specs(tm) + [
                  pl.BlockSpec((tm, D), lambda i: (i, 0)),
                  pl.BlockSpec((tm, P_NSEL), lambda i: (i, 0)),
                  pl.BlockSpec(memory_space=pltpu.VMEM),
                  pl.BlockSpec((NCOL, P_NSEL), lambda i: (0, 0))],
        out_specs=pl.BlockSpec((tm, P_NSEL), lambda i: (i, 0)),
        out_shape=jax.ShapeDtypeStruct((n, P_NSEL), F32),
        scratch_shapes=[pltpu.VMEM((NROWS, LANES), jnp.int32)] * PEER_BUFS + [pltpu.VMEM((tm, NCOL), F32)],
        compiler_params=_params(("arbitrary",)),
        name="peer_u",
    )(*_deal_offsets(eoff), h2r, gate, tab, _group_matrix())


def _peer_v_kernel(*refs, tm):
    off_refs = refs[:OFF_WAYS]
    w_ref, tab_ref, rep_ref, x1_ref, gf_ref, y_ref = refs[OFF_WAYS:OFF_WAYS + 6]
    bufs, w8_ref, po_ref = refs[OFF_WAYS + 6:-2], refs[-2], refs[-1]
    mask = _diag_mask()
    w8_ref[...] = _dot(w_ref[...], rep_ref[...], precision=HIGHEST)
    row8 = lax.broadcasted_iota(jnp.int32, (2 * SUB, LANES), 0)

    def gather(t, g_ref):
        return _gather_rows(off_refs, tab_ref, g_ref, t)

    def compute(t, g_ref):
        g16 = pltpu.bitcast(g_ref[...], BF16)
        wexp = jnp.where(mask, jnp.broadcast_to(w8_ref[pl.ds(t, 1), :], (2 * SUB, NCOL)), 0.0)
        o = _dot(_split_bf16(wexp), g16)
        o8 = o[0:2 * SUB] + o[2 * SUB:]
        pieces = [jnp.sum(jnp.where(row8 == j, o8, 0.0), axis=0, keepdims=True) for j in range(2 * SUB)]
        po_ref[pl.ds(t, 1), :] = jnp.concatenate(pieces, axis=1)

    _buffered_token_loop(tm, gather, compute, bufs)
    x = x1_ref[...] + po_ref[...]
    y_ref[...] = x * lax.rsqrt(jnp.mean(x * x, axis=-1, keepdims=True) + EPS) * gf_ref[...]


def _peer_v(eoff, w, tab, x1, g_final, tm=256):
    n = eoff.shape[0]
    return pl.pallas_call(
        functools.partial(_peer_v_kernel, tm=tm),
        grid=(n // tm,),
        in_specs=_off_specs(tm) + [
                  pl.BlockSpec((tm, P_NSEL), lambda i: (i, 0)),
                  pl.BlockSpec(memory_space=pltpu.VMEM),
                  pl.BlockSpec((P_NSEL, NCOL), lambda i: (0, 0)),
                  pl.BlockSpec((tm, D), lambda i: (i, 0)),
                  pl.BlockSpec((1, D), lambda i: (0, 0))],
        out_specs=pl.BlockSpec((tm, D), lambda i: (i, 0)),
        out_shape=jax.ShapeDtypeStruct((n, D), F32),
        scratch_shapes=[pltpu.VMEM((NROWS, LANES), jnp.int32)] * PEER_BUFS
        + [pltpu.VMEM((tm, NCOL), F32), pltpu.VMEM((tm, D), F32)],
        compiler_params=_params(("arbitrary",)),
        name="peer_v",
    )(*_deal_offsets(eoff), w, tab, _group_matrix().T, x1, g_final.reshape(1, D))


def _pack_w_in(w_in):
    w = jnp.concatenate([w_in[:, 2048:3584], w_in[:, :2048], w_in[:, 3584:4096], w_in[:, 4112:6160],
                         w_in[:, 4096:4112], jnp.zeros((D, PW - C_SMALL - 16), w_in.dtype)], axis=1)
    return w.astype(BF16)


def _encoder(x, wts):
    b, s, _ = x.shape
    n = b * s
    x2 = x.reshape(n, D)
    p2 = _in_proj(x2, wts["g_mix"], wts["w_cat"])
    p3 = p2.reshape(b, s, PW)
    oa = _retention(p3, wts["ret_gn"]).reshape(n, HW)
    qkv, sm = _gdn_prep(p3, wts["dn_conv"], wts["dn_a_log"], wts["dn_dt_bias"])
    o_f, o_b = _gdn_scan(_gdn_terms(qkv, sm), nb=4 if b % 4 == 0 else 1)
    x1 = _merge(x2, oa, o_f.reshape(n, HW), o_b.reshape(n, HW), p2, wts["dn_norm"],
                wts["wa"], wts["wb"], wts["wo"])
    h2, eidx, gate = _route(x1, wts["g_ffn"], wts["wq"], wts["k1"], wts["k2"])
    w = _peer_u(eidx, h2, gate, wts["tab_u"])
    return _peer_v(eidx, w, wts["tab_v"], x1, wts["g_final"]).reshape(b, s, D)


def kernel(x_prompt, x_sample, g_mix, w_in, ret_gn, dn_conv, dn_a_log, dn_dt_bias, dn_norm, w_branch_a, w_branch_b, w_out, g_ffn, peer_wq, peer_k1, peer_k2, peer_u, peer_v, g_final):
    wts = dict(
        g_mix=g_mix[0], w_cat=_pack_w_in(w_in[0]), ret_gn=ret_gn[0], dn_conv=dn_conv[0],
        dn_a_log=dn_a_log[0], dn_dt_bias=dn_dt_bias[0], dn_norm=dn_norm[0],
        wa=w_branch_a[0].astype(BF16), wb=w_branch_b[0].astype(BF16), wo=w_out[0].astype(BF16),
        g_ffn=g_ffn[0], wq=peer_wq[0].astype(BF16), k1=peer_k1[0].astype(BF16), k2=peer_k2[0].astype(BF16),
        tab_u=_pack_table(peer_u[0]), tab_v=_pack_table(peer_v[0]), g_final=g_final)
    return (_encoder(x_prompt, wts), _encoder(x_sample, wts))
```

```python
import functools

import jax
import jax.numpy as jnp
import numpy as np
from jax import lax
from jax.experimental import pallas as pl
from jax.experimental.pallas import tpu as pltpu

F32 = jnp.float32
BF16 = jnp.bfloat16
HIGHEST = lax.Precision.HIGHEST

D = 1024
EPS = 1e-6
NH = 4
HD = 128
HW = NH * HD
ROPE_BASE = 10000.0
DN_CONV = 5
DN_C = 64
RET_C = 512
P_HEADS = 8
P_DK = 256
P_NKEYS = 128
P_TOPK = 16
P_NSEL = P_HEADS * P_TOPK
LANES = 128
SUB = 4
V7X_VMEM_BYTES = 64 * 1024 * 1024
VMEM_LIMIT = V7X_VMEM_BYTES * 7 // 8

C_DNQKV = 0
C_RET = 1536
C_Z = 3584
C_MA = 4096
C_MB = 5120
C_SMALL = 6144
PW = 6272
PCH = 896

TM_IN_PROJ = 256
TM_GDN_PREP = 512
TM_MERGE = 512
TM_ROUTE = LANES
TM_PEER = 256
GDN_TERMS_CHUNKS = 4
GDN_SCAN_SEQS = 4


def _dot(a, b, **kw):
    return jnp.dot(a, b, preferred_element_type=F32, **kw)


def _dot_nt(a, b, **kw):
    return lax.dot_general(a, b, (((1,), (1,)), ((), ())), preferred_element_type=F32, **kw)


def _params(sem=None):
    return pltpu.CompilerParams(dimension_semantics=sem, vmem_limit_bytes=VMEM_LIMIT)


def _in_proj_kernel(x_ref, g_ref, w_ref, p_ref):
    x = x_ref[...]
    h = x * lax.rsqrt(jnp.mean(x * x, axis=-1, keepdims=True) + EPS) * g_ref[...]
    h = h.astype(BF16)
    for c in range(0, PW, PCH):
        p_ref[:, c:c + PCH] = _dot(h, w_ref[:, c:c + PCH])


def _in_proj(x2, g, w_cat, tm=TM_IN_PROJ):
    n = x2.shape[0]
    return pl.pallas_call(
        _in_proj_kernel,
        grid=(n // tm,),
        in_specs=[pl.BlockSpec((tm, D), lambda i: (i, 0)),
                  pl.BlockSpec((1, D), lambda i: (0, 0)),
                  pl.BlockSpec((D, PW), lambda i: (0, 0))],
        out_specs=pl.BlockSpec((tm, PW), lambda i: (i, 0)),
        out_shape=jax.ShapeDtypeStruct((n, PW), F32),
        compiler_params=_params(("parallel",)),
        name="in_proj",
    )(x2, g.reshape(1, D), w_cat)


def _ret_kernel(q_ref, k_ref, v_ref, gt_ref, cos_ref, sin_ref, dmat_ref, vec_ref, cdec_ref, gn_ref,
                o_ref, run_ref, stash_ref):
    ph = pl.program_id(1)
    c = pl.program_id(2)
    nc = pl.num_programs(2)

    @pl.when(c == 0)
    def _():
        run_ref[...] = jnp.zeros_like(run_ref)

    cos2 = cos_ref[...]
    sinm = sin_ref[...]

    def rot(t):
        return t * cos2 + pltpu.roll(t, HD // 2, axis=1) * sinm

    def state_update(h, kh, vh, zeta, dec_row):
        kz = (kh * zeta).T.astype(BF16)
        contrib = _dot(kz, vh.astype(BF16))
        run_ref[h] = run_ref[h] * cdec_ref[dec_row:dec_row + 1, :] + contrib

    @pl.when(ph == 0)
    def _():
        ci = nc - 1 - c
        for h in range(NH):
            sl = slice(h * HD, (h + 1) * HD)
            kh = rot(k_ref[0, :, sl])
            stash_ref[ci, h] = run_ref[h]
            state_update(h, kh, v_ref[0, :, sl], vec_ref[3, h], NH + h)

    @pl.when(ph == 1)
    def _():
        for h in range(NH):
            sl = slice(h * HD, (h + 1) * HD)
            qh = rot(q_ref[0, :, sl]) * (HD ** -0.5)
            kh = rot(k_ref[0, :, sl])
            vh = v_ref[0, :, sl]
            vb = vh.astype(BF16)
            scores = _dot_nt(qh.astype(BF16), kh.astype(BF16)) * dmat_ref[h]
            o = _dot(scores.astype(BF16), vb)
            o += _dot((qh * vec_ref[0, h]).astype(BF16), run_ref[h].astype(BF16))
            o += _dot((qh * vec_ref[1, h]).astype(BF16), stash_ref[c, h].astype(BF16))
            state_update(h, kh, vh, vec_ref[2, h], h)
            mu = jnp.mean(o, axis=-1, keepdims=True)
            oc = o - mu
            var = jnp.mean(oc * oc, axis=-1, keepdims=True)
            on = oc * lax.rsqrt(var + EPS) * gn_ref[:, sl]
            gate = gt_ref[0, :, sl]
            o_ref[0, :, sl] = (on * (gate * jax.nn.sigmoid(gate))).astype(o_ref.dtype)


def _ret_tables(s):
    half = HD // 2
    inv_freq = 1.0 / (ROPE_BASE ** jnp.linspace(0.0, 1.0, half, dtype=F32))
    ang = jnp.arange(s, dtype=F32)[:, None] * inv_freq[None, :]
    cos, sin = jnp.cos(ang), jnp.sin(ang)
    cos2 = jnp.concatenate([cos, cos], axis=-1)
    sinm = jnp.concatenate([-sin, sin], axis=-1)
    lg = jnp.log(1.0 - 2.0 ** (-5.0 - jnp.arange(NH, dtype=F32)))
    lgb = lg[::-1]
    c = RET_C
    pos = jnp.arange(c, dtype=F32)
    diff = pos[:, None] - pos[None, :]
    dmat = jnp.where(diff >= 0, jnp.exp(lg[:, None, None] * jnp.maximum(diff, 0.0)),
                     jnp.exp(lgb[:, None, None] * jnp.maximum(-diff, 0.0)))
    xi_f = jnp.exp(lg[:, None] * (pos + 1.0))
    xi_b = jnp.exp(lgb[:, None] * (c - pos))
    zeta_f = jnp.exp(lg[:, None] * (c - 1 - pos))
    zeta_b = jnp.exp(lgb[:, None] * pos)
    vec = jnp.broadcast_to(jnp.stack([xi_f, xi_b, zeta_f, zeta_b])[..., None], (4, NH, c, LANES))
    cdec = jnp.concatenate([jnp.exp(lg * c), jnp.exp(lgb * c)])
    cdec = jnp.broadcast_to(cdec[:, None], (2 * NH, LANES))
    return cos2, sinm, dmat, vec, cdec


def _retention(p3, ret_gn):
    b, s, _ = p3.shape
    c = RET_C
    nc = s // c
    cos2, sinm, dmat, vec, cdec = _ret_tables(s)

    def chunk(ph, ci):
        return jnp.where(ph == 0, nc - 1 - ci, ci)

    def pspec(col):
        return pl.BlockSpec((1, c, HW), lambda bi, ph, ci: (bi, chunk(ph, ci), col))

    base = C_RET // HW
    return pl.pallas_call(
        _ret_kernel,
        grid=(b, 2, nc),
        in_specs=[pspec(base), pspec(base + 1), pspec(base + 2), pspec(base + 3),
                  pl.BlockSpec((c, HD), lambda bi, ph, ci: (chunk(ph, ci), 0)),
                  pl.BlockSpec((c, HD), lambda bi, ph, ci: (chunk(ph, ci), 0)),
                  pl.BlockSpec((NH, c, c), lambda bi, ph, ci: (0, 0, 0)),
                  pl.BlockSpec((4, NH, c, LANES), lambda bi, ph, ci: (0, 0, 0, 0)),
                  pl.BlockSpec((2 * NH, LANES), lambda bi, ph, ci: (0, 0)),
                  pl.BlockSpec((1, HW), lambda bi, ph, ci: (0, 0))],
        out_specs=pl.BlockSpec((1, c, HW), lambda bi, ph, ci: (bi, ci * ph, 0)),
        out_shape=jax.ShapeDtypeStruct((b, s, HW), BF16),
        scratch_shapes=[pltpu.VMEM((NH, HD, HD), F32), pltpu.VMEM((nc, NH, HD, HD), F32)],
        compiler_params=_params(("parallel", "arbitrary", "arbitrary")),
        name="retention",
    )(p3, p3, p3, p3, cos2, sinm, dmat, vec, cdec, ret_gn.reshape(1, HW))


def _gdn_prep_kernel(prev_ref, cur_ref, next_ref, sm_ref, cw_ref, alog_ref, dtb_ref, qkv_ref, smo_ref, *, tc):
    i = pl.program_id(1)
    last = pl.num_programs(1) - 1
    prev = jnp.where(i == 0, 0.0, prev_ref[0])
    nxt = jnp.where(i == last, 0.0, next_ref[0])
    ext = jnp.concatenate([prev, cur_ref[0], nxt], axis=0)
    nrow = tc + 16
    half = DN_CONV // 2
    y = jnp.zeros((tc, 3 * HW), F32)
    for j in range(DN_CONV):
        shifted = ext if j == half else pltpu.roll(ext, (half - j) % nrow, axis=0)
        y = y + shifted[8:8 + tc] * cw_ref[j:j + 1, :]
    y = y * jax.nn.sigmoid(y)
    for h in range(2 * NH):
        sl = slice(h * HD, (h + 1) * HD)
        t = y[:, sl]
        t = t * lax.rsqrt(jnp.sum(t * t, axis=-1, keepdims=True) + EPS)
        if h < NH:
            t = t * (HD ** -0.5)
        qkv_ref[0, :, sl] = t
    qkv_ref[0, :, 2 * HW:] = y[:, 2 * HW:]
    a = sm_ref[0]
    lane = lax.broadcasted_iota(jnp.int32, a.shape, 1)
    z = a + dtb_ref[...]
    softplus = jnp.maximum(z, 0.0) + jnp.log1p(jnp.exp(-jnp.abs(z)))
    g = -jnp.exp(alog_ref[...]) * softplus
    smo_ref[0] = jnp.where(lane < 2 * NH, jax.nn.sigmoid(a), g)


def _gdn_prep(p3, conv_w, a_log, dt_bias, tc=TM_GDN_PREP):
    b, s, _ = p3.shape
    nt = s // tc
    r8 = tc // 8
    w3 = 3 * HW
    zeros8 = jnp.zeros((2 * NH,), F32)
    pad = jnp.zeros((LANES - 4 * NH,), F32)
    alog = jnp.concatenate([zeros8, a_log[0], a_log[1], pad]).reshape(1, LANES)
    dtb = jnp.concatenate([zeros8, dt_bias[0], dt_bias[1], pad]).reshape(1, LANES)
    return pl.pallas_call(
        functools.partial(_gdn_prep_kernel, tc=tc),
        grid=(b, nt),
        in_specs=[pl.BlockSpec((1, 8, w3), lambda bi, i: (bi, jnp.maximum(i * r8 - 1, 0), 0)),
                  pl.BlockSpec((1, tc, w3), lambda bi, i: (bi, i, 0)),
                  pl.BlockSpec((1, 8, w3), lambda bi, i: (bi, jnp.minimum((i + 1) * r8, s // 8 - 1), 0)),
                  pl.BlockSpec((1, tc, LANES), lambda bi, i: (bi, i, C_SMALL // LANES)),
                  pl.BlockSpec((DN_CONV, w3), lambda bi, i: (0, 0)),
                  pl.BlockSpec((1, LANES), lambda bi, i: (0, 0)),
                  pl.BlockSpec((1, LANES), lambda bi, i: (0, 0))],
        out_specs=[pl.BlockSpec((1, tc, w3), lambda bi, i: (bi, i, 0)),
                   pl.BlockSpec((1, tc, LANES), lambda bi, i: (bi, i, 0))],
        out_shape=[jax.ShapeDtypeStruct((b, s, w3), F32), jax.ShapeDtypeStruct((b, s, LANES), F32)],
        compiler_params=_params(("parallel", "parallel")),
        name="gdn_prep",
    )(p3, p3, p3, p3, conv_w, alog, dtb)


def _gdn_terms_kernel(qkv_ref, sm_ref, *out_refs, nchunk):
    c = DN_C
    row = lax.broadcasted_iota(jnp.int32, (c, c), 0)
    col = lax.broadcasted_iota(jnp.int32, (c, c), 1)
    lane = lax.broadcasted_iota(jnp.int32, (c, LANES), 1)
    gt_ref = out_refs[-1]
    ids, qn, kn, vv, gcum, beta, lmat, strict = [], [], [], [], [], [], [], []
    for ci in range(nchunk):
        rows = slice(ci * c, (ci + 1) * c)
        sm = sm_ref[0, rows, :]
        for d in range(2):
            tri = jnp.where((row <= col) if d else (row >= col), 1.0, 0.0)
            gc_all = _dot(tri, sm, precision=HIGHEST)
            gc_t = gc_all.T
            incl = (row <= col) if d else (row >= col)
            for h in range(NH):
                lg = 2 * NH + NH * d + h
                ids.append((ci, d, h))
                qn.append(qkv_ref[0, rows, h * HD:(h + 1) * HD])
                kn.append(qkv_ref[0, rows, HW + h * HD:HW + (h + 1) * HD])
                vv.append(qkv_ref[0, rows, 2 * HW + h * HD:2 * HW + (h + 1) * HD])
                beta.append(jnp.sum(jnp.where(lane == NH * d + h, sm, 0.0), axis=-1, keepdims=True))
                g = jnp.sum(jnp.where(lane == lg, gc_all, 0.0), axis=-1, keepdims=True)
                gcum.append(g)
                lmat.append(jnp.where(incl, jnp.exp(jnp.where(incl, g - gc_t[lg:lg + 1, :], 0.0)), 0.0))
                strict.append((row < col) if d else (row > col))
    n = len(ids)
    rng = range(n)
    kb = [kn[i] * beta[i] for i in rng]
    knb = [kn[i].astype(BF16) for i in rng]
    kk = [_dot_nt(kb[i].astype(BF16), knb[i]) for i in rng]
    a = [jnp.where(strict[i], kk[i] * lmat[i], 0.0) for i in rng]
    ab = [a[i].astype(BF16) for i in rng]
    p = [_dot(ab[i], ab[i]) for i in rng]
    m = [-a[i] for i in rng]
    steps = int(np.log2(c)) - 1
    for it in range(steps):
        pb = [p[i].astype(BF16) for i in rng]
        mp = [_dot(m[i].astype(BF16), pb[i]) for i in rng]
        m = [m[i] + p[i] + mp[i] for i in rng]
        if it + 1 < steps:
            p = [_dot(pb[i], pb[i]) for i in rng]
    eg = [jnp.exp(gcum[i]) for i in rng]
    rhs = [jnp.concatenate([vv[i] * beta[i], kb[i] * eg[i]], axis=-1) for i in rng]
    corr = [_dot(m[i].astype(BF16), rhs[i].astype(BF16)) for i in rng]
    attn = [_dot_nt(qn[i].astype(BF16), knb[i]) for i in rng]
    for i, (ci, d, h) in enumerate(ids):
        u_ref, w_ref, qd_ref, kdt_ref, att_ref = out_refs[5 * d:5 * d + 5]
        rows = slice(ci * c, (ci + 1) * c)
        sl = slice(h * HD, (h + 1) * HD)
        sol = rhs[i] + corr[i]
        gl = gcum[i][0:1, :] if d else gcum[i][c - 1:c, :]
        u_ref[0, rows, sl] = sol[:, :HD]
        w_ref[0, rows, sl] = sol[:, HD:].astype(BF16)
        qd_ref[0, rows, sl] = (qn[i] * eg[i]).astype(BF16)
        kdt_ref[0, ci, h] = (kn[i] * jnp.exp(gl - gcum[i])).T.astype(BF16)
        att_ref[0, ci, h] = (attn[i] * lmat[i]).astype(BF16)
        gt_ref[0, ci, NH * d + h:NH * d + h + 1, :] = jnp.broadcast_to(jnp.exp(gl), (1, LANES))


def _gdn_terms(qkv, sm, nchunk=GDN_TERMS_CHUNKS):
    b, s, w3 = qkv.shape
    c = DN_C
    nc = s // c
    r = nchunk * c
    tok = lambda dt: (pl.BlockSpec((1, r, HW), lambda bi, i: (bi, i, 0)), jax.ShapeDtypeStruct((b, s, HW), dt))
    kdt = (pl.BlockSpec((1, nchunk, NH, HD, c), lambda bi, i: (bi, i, 0, 0, 0)),
           jax.ShapeDtypeStruct((b, nc, NH, HD, c), BF16))
    att = (pl.BlockSpec((1, nchunk, NH, c, c), lambda bi, i: (bi, i, 0, 0, 0)),
           jax.ShapeDtypeStruct((b, nc, NH, c, c), BF16))
    gt = (pl.BlockSpec((1, nchunk, 2 * NH, LANES), lambda bi, i: (bi, i, 0, 0)),
          jax.ShapeDtypeStruct((b, nc, 2 * NH, LANES), F32))
    outs = [tok(F32), tok(BF16), tok(BF16), kdt, att] * 2 + [gt]
    return pl.pallas_call(
        functools.partial(_gdn_terms_kernel, nchunk=nchunk),
        grid=(b, nc // nchunk),
        in_specs=[pl.BlockSpec((1, r, w3), lambda bi, i: (bi, i, 0)),
                  pl.BlockSpec((1, r, LANES), lambda bi, i: (bi, i, 0))],
        out_specs=[o[0] for o in outs],
        out_shape=[o[1] for o in outs],
        compiler_params=_params(("parallel", "parallel")),
        name="gdn_terms",
    )(qkv, sm)


def _gdn_scan_kernel(*refs, nb):
    (uf, wf, qdf, kdtf, attf, gtf, ub, wb, qdb, kdtb, attb, gtb, of_ref, ob_ref, s_ref) = refs

    @pl.when(pl.program_id(1) == 0)
    def _():
        s_ref[...] = jnp.zeros_like(s_ref)

    dirs = ((uf, wf, qdf, kdtf, attf, gtf, of_ref), (ub, wb, qdb, kdtb, attb, gtb, ob_ref))
    ids = [(bi, d, h) for bi in range(nb) for d in range(2) for h in range(NH)]
    sl = [slice(h * HD, (h + 1) * HD) for h in range(NH)]
    s = [s_ref[bi, d, h] for bi, d, h in ids]
    sb = [t.astype(BF16) for t in s]
    ws = [_dot(dirs[d][1][bi, :, sl[h]], sb[i]) for i, (bi, d, h) in enumerate(ids)]
    qs = [_dot(dirs[d][2][bi, :, sl[h]], sb[i]) for i, (bi, d, h) in enumerate(ids)]
    vnb = [(dirs[d][0][bi, :, sl[h]] - ws[i]).astype(BF16) for i, (bi, d, h) in enumerate(ids)]
    av = [_dot(dirs[d][4][bi, 0, h], vnb[i]) for i, (bi, d, h) in enumerate(ids)]
    kv = [_dot(dirs[d][3][bi, 0, h], vnb[i]) for i, (bi, d, h) in enumerate(ids)]
    for i, (bi, d, h) in enumerate(ids):
        dirs[d][6][bi, :, sl[h]] = qs[i] + av[i]
        gt = dirs[d][5][bi, 0, NH * d + h:NH * d + h + 1, :]
        s_ref[bi, d, h] = s[i] * gt + kv[i]


def _gdn_scan(terms, nb):
    uf, wf, qdf, kdtf, attf, ub, wb, qdb, kdtb, attb, gt = terms
    b, s, _ = uf.shape
    c = DN_C
    nc = s // c
    fwd = lambda bi, i: i
    bwd = lambda bi, i: nc - 1 - i

    def specs(pos):
        return [pl.BlockSpec((nb, c, HW), lambda bi, i: (bi, pos(bi, i), 0)),
                pl.BlockSpec((nb, c, HW), lambda bi, i: (bi, pos(bi, i), 0)),
                pl.BlockSpec((nb, c, HW), lambda bi, i: (bi, pos(bi, i), 0)),
                pl.BlockSpec((nb, 1, NH, HD, c), lambda bi, i: (bi, pos(bi, i), 0, 0, 0)),
                pl.BlockSpec((nb, 1, NH, c, c), lambda bi, i: (bi, pos(bi, i), 0, 0, 0)),
                pl.BlockSpec((nb, 1, 2 * NH, LANES), lambda bi, i: (bi, pos(bi, i), 0, 0))]

    return pl.pallas_call(
        functools.partial(_gdn_scan_kernel, nb=nb),
        grid=(b // nb, nc),
        in_specs=specs(fwd) + specs(bwd),
        out_specs=[pl.BlockSpec((nb, c, HW), lambda bi, i: (bi, i, 0)),
                   pl.BlockSpec((nb, c, HW), lambda bi, i: (bi, nc - 1 - i, 0))],
        out_shape=[jax.ShapeDtypeStruct((b, s, HW), F32), jax.ShapeDtypeStruct((b, s, HW), F32)],
        scratch_shapes=[pltpu.VMEM((nb, 2, NH, HD, HD), F32)],
        compiler_params=_params(("parallel", "arbitrary")),
        name="gdn_scan",
    )(uf, wf, qdf, kdtf, attf, gt, ub, wb, qdb, kdtb, attb, gt)


def _merge_kernel(x_ref, oa_ref, of_ref, ob_ref, z_ref, ma_ref, mb_ref, ng_ref, wa_ref, wb_ref, wo_ref, x1_ref):
    ob_parts = []
    for h in range(NH):
        sl = slice(h * HD, (h + 1) * HD)
        o = of_ref[:, sl] + ob_ref[:, sl]
        o = o * lax.rsqrt(jnp.mean(o * o, axis=-1, keepdims=True) + EPS) * ng_ref[...]
        z = z_ref[:, sl]
        ob_parts.append((o * (z * jax.nn.sigmoid(z))).astype(BF16))
    ob = jnp.concatenate(ob_parts, axis=-1)
    ya = _dot(oa_ref[...], wa_ref[...])
    yb = _dot(ob, wb_ref[...])
    merged = jax.nn.sigmoid(ma_ref[...]) * ya + jax.nn.sigmoid(mb_ref[...]) * yb
    x1_ref[...] = x_ref[...] + _dot(merged.astype(BF16), wo_ref[...])


def _merge(x2, oa, o_f, o_b, p2, norm_g, wa, wb, wo, tm=TM_MERGE):
    n = x2.shape[0]
    const = lambda i: (0, 0)
    return pl.pallas_call(
        _merge_kernel,
        grid=(n // tm,),
        in_specs=[pl.BlockSpec((tm, D), lambda i: (i, 0)),
                  pl.BlockSpec((tm, HW), lambda i: (i, 0)),
                  pl.BlockSpec((tm, HW), lambda i: (i, 0)),
                  pl.BlockSpec((tm, HW), lambda i: (i, 0)),
                  pl.BlockSpec((tm, HW), lambda i: (i, C_Z // HW)),
                  pl.BlockSpec((tm, D), lambda i: (i, C_MA // D)),
                  pl.BlockSpec((tm, D), lambda i: (i, C_MB // D)),
                  pl.BlockSpec((1, HD), const),
                  pl.BlockSpec((HW, D), const),
                  pl.BlockSpec((HW, D), const),
                  pl.BlockSpec((D, D), const)],
        out_specs=pl.BlockSpec((tm, D), lambda i: (i, 0)),
        out_shape=jax.ShapeDtypeStruct((n, D), F32),
        compiler_params=_params(("parallel",)),
        name="merge",
    )(x2, oa, o_f, o_b, p2, p2, p2, norm_g.reshape(1, HD), wa, wb, wo)


def _top_rows(arrays, payload=None):
    r = arrays[0].shape[0]
    row = lax.broadcasted_iota(jnp.int32, arrays[0].shape, 0).astype(F32)
    arrays = list(arrays)
    out = [([], []) for _ in arrays]
    for _ in range(P_TOPK):
        for a, s in enumerate(arrays):
            m = jnp.max(s, axis=0, keepdims=True)
            first = jnp.min(jnp.where(s == m, row, float(r)), axis=0, keepdims=True)
            sel = row == first
            out[a][0].append(m)
            out[a][1].append(first if payload is None
                             else jnp.max(jnp.where(sel, payload, -1.0), axis=0, keepdims=True))
            arrays[a] = jnp.where(sel, -jnp.inf, s)
    return out


def _stack_rows(rows):
    t = rows[0].shape[1]
    row = lax.broadcasted_iota(jnp.int32, (len(rows), t), 0)
    out = jnp.zeros((len(rows), t), F32)
    for i, r in enumerate(rows):
        out = jnp.where(row == i, r, out)
    return out


def _pair_candidates(v1, i1, v2, i2):
    v1s, i1s, v2s, i2s = _stack_rows(v1), _stack_rows(i1), _stack_rows(v2), _stack_rows(i2)
    brow = lax.broadcasted_iota(jnp.int32, (8, v2s.shape[1]), 0)
    cand = [v1[0] + v2s]
    cidx = [i1[0] * P_NKEYS + i2s]
    for a in range(1, 8):
        ok = brow < P_TOPK // (a + 1)
        cand.append(jnp.where(ok, v1[a] + v2s[0:8], -jnp.inf))
        cidx.append(i1[a] * P_NKEYS + i2s[0:8])
    cand.append(v1s[8:] + v2[0])
    cidx.append(i1s[8:] * P_NKEYS + i2[0])
    return jnp.concatenate(cand, axis=0), jnp.concatenate(cidx, axis=0)


def _route_kernel(x1_ref, g_ref, wq_ref, k1_ref, k2_ref, h2_ref, eidx_ref, gate_ref):
    x = x1_ref[...]
    h2 = x * lax.rsqrt(jnp.mean(x * x, axis=-1, keepdims=True) + EPS) * g_ref[...]
    h2_ref[...] = h2
    qp = _dot(h2.astype(BF16), wq_ref[...])
    gates, eidxs = [], []
    for h in range(P_HEADS):
        q1 = qp[:, h * P_DK:h * P_DK + P_DK // 2].astype(BF16)
        q2 = qp[:, h * P_DK + P_DK // 2:(h + 1) * P_DK].astype(BF16)
        s1 = _dot_nt(k1_ref[h], q1)
        s2 = _dot_nt(k2_ref[h], q2)
        (v1, i1), (v2, i2) = _top_rows([s1, s2])
        cand, cidx = _pair_candidates(v1, i1, v2, i2)
        (sc, ei), = _top_rows([cand], cidx)
        ex = jnp.exp(_stack_rows(sc) - sc[0])
        gates.append(ex / jnp.sum(ex, axis=0, keepdims=True))
        eidxs.append(_stack_rows(ei) * float(SUB))
    gate_ref[...] = jnp.concatenate(gates, axis=0).T
    c = lax.broadcasted_iota(jnp.int32, (P_NSEL, P_NSEL), 0)
    s = lax.broadcasted_iota(jnp.int32, (P_NSEL, P_NSEL), 1)
    perm = jnp.where(s == OFF_WAYS * (c % OFF_PER) + c // OFF_PER, 1.0, 0.0)
    dealt = _dot(perm, jnp.concatenate(eidxs, axis=0), precision=HIGHEST)
    eidx_ref[...] = dealt.T.astype(jnp.int32)


def _route(x1, g_ffn, wq, k1, k2, tm=TM_ROUTE):
    n = x1.shape[0]
    return pl.pallas_call(
        _route_kernel,
        grid=(n // tm,),
        in_specs=[pl.BlockSpec((tm, D), lambda i: (i, 0)),
                  pl.BlockSpec((1, D), lambda i: (0, 0)),
                  pl.BlockSpec((D, P_HEADS * P_DK), lambda i: (0, 0)),
                  pl.BlockSpec((P_HEADS, P_NKEYS, P_DK // 2), lambda i: (0, 0, 0)),
                  pl.BlockSpec((P_HEADS, P_NKEYS, P_DK // 2), lambda i: (0, 0, 0))],
        out_specs=[pl.BlockSpec((tm, D), lambda i: (i, 0)),
                   pl.BlockSpec((tm, P_NSEL), lambda i: (i, 0)),
                   pl.BlockSpec((tm, P_NSEL), lambda i: (i, 0))],
        out_shape=[jax.ShapeDtypeStruct((n, D), F32),
                   jax.ShapeDtypeStruct((n, P_NSEL), jnp.int32),
                   jax.ShapeDtypeStruct((n, P_NSEL), F32)],
        compiler_params=_params(("parallel",)),
        name="route",
    )(x1, g_ffn.reshape(1, D), wq, k1, k2)


def _pack_table(t):
    bits = lax.bitcast_convert_type(t.astype(BF16), jnp.uint16).astype(jnp.uint32)
    packed = bits[:, :D // 2] | (bits[:, D // 2:] << 16)
    return lax.bitcast_convert_type(packed, jnp.int32).reshape(t.shape[0] * SUB, LANES)


PEER_UNROLL = 16
NROWS = P_NSEL * SUB
NCOL = 2 * NROWS


OFF_WAYS = 8
OFF_PER = P_NSEL // OFF_WAYS


def _deal_offsets(eoff):
    n = eoff.shape[0]
    return [eoff[:, OFF_PER * j:OFF_PER * (j + 1)].reshape(n * OFF_PER) for j in range(OFF_WAYS)]


def _gather_rows(off_refs, tab_ref, g_ref, t):
    off = 0
    for i in range(OFF_PER):
        pos = t * OFF_PER + i
        for j, off_ref in enumerate(off_refs):
            k = OFF_WAYS * i + j
            off = pl.multiple_of(off_ref[pos], SUB)
            g_ref[SUB * k:SUB * (k + 1), :] = tab_ref[pl.ds(off, SUB), :]
    return off


PEER_BUFS = 2


def _buffered_token_loop(tm, gather, compute, bufs):
    lag = len(bufs) - 1
    last = 0
    for j in range(lag):
        last = gather(j + jnp.minimum(last, 0), bufs[j])

    def body(i, carry):
        t = PEER_UNROLL * i
        last = 0
        for j in range(PEER_UNROLL):
            compute(t + j, bufs[j % len(bufs)])
            nxt = jnp.minimum(t + j + lag, tm - 1) + jnp.minimum(last, 0)
            last = gather(nxt, bufs[(j + lag) % len(bufs)])
        return carry

    lax.fori_loop(0, tm // PEER_UNROLL, body, 0)


def _diag_mask():
    j = lax.broadcasted_iota(jnp.int32, (2 * SUB, NCOL), 0)
    c = lax.broadcasted_iota(jnp.int32, (2 * SUB, NCOL), 1)
    return (c & 7) == ((j & 3) * 2 + (j >> 2))


def _split_bf16(a):
    hi = a.astype(BF16)
    lo = (a - hi.astype(F32)).astype(BF16)
    return jnp.concatenate([hi, lo], axis=0)


def _peer_u_kernel(*refs, tm):
    off_refs = refs[:OFF_WAYS]
    x_ref, gate_ref, tab_ref, sel_ref, w_ref = refs[OFF_WAYS:OFF_WAYS + 5]
    bufs, r_ref = refs[OFF_WAYS + 5:-1], refs[-1]
    mask = _diag_mask()

    def gather(t, g_ref):
        return _gather_rows(off_refs, tab_ref, g_ref, t)

    def compute(t, g_ref):
        g16 = pltpu.bitcast(g_ref[...], BF16)
        x8 = x_ref[pl.ds(t, 1), :].reshape(2 * SUB, LANES)
        o = _dot_nt(_split_bf16(x8), g16)
        o8 = o[0:2 * SUB] + o[2 * SUB:]
        r_ref[pl.ds(t, 1), :] = jnp.sum(jnp.where(mask, o8, 0.0), axis=0, keepdims=True)

    _buffered_token_loop(tm, gather, compute, bufs)
    pre = _dot(r_ref[...], sel_ref[...], precision=HIGHEST)
    act = 0.5 * pre * (1.0 + lax.erf(pre * (2.0 ** -0.5)))
    w_ref[...] = gate_ref[...] * act


def _group_matrix():
    c = jnp.arange(NCOL, dtype=jnp.int32)[:, None] // 8
    k = jnp.arange(P_NSEL, dtype=jnp.int32)[None, :]
    return (c == k).astype(F32)


def _off_specs(tm):
    return [pl.BlockSpec((tm * OFF_PER,), lambda i: (i,), memory_space=pltpu.SMEM) for _ in range(OFF_WAYS)]


def _peer_u(eoff, h2, gate, tab, tm=TM_PEER):
    n = eoff.shape[0]
    return pl.pallas_call(
        functools.partial(_peer_u_kernel, tm=tm),
        grid=(n // tm,),
        in_specs=_off_specs(tm) + [
                  pl.BlockSpec((tm, D), lambda i: (i, 0)),
                  pl.BlockSpec((tm, P_NSEL), lambda i: (i, 0)),
                  pl.BlockSpec(memory_space=pltpu.VMEM),
                  pl.BlockSpec((NCOL, P_NSEL), lambda i: (0, 0))],
        out_specs=pl.BlockSpec((tm, P_NSEL), lambda i: (i, 0)),
        out_shape=jax.ShapeDtypeStruct((n, P_NSEL), F32),
        scratch_shapes=[pltpu.VMEM((NROWS, LANES), jnp.int32)] * PEER_BUFS + [pltpu.VMEM((tm, NCOL), F32)],
        compiler_params=_params(("arbitrary",)),
        name="peer_u",
    )(*_deal_offsets(eoff), h2, gate, tab, _group_matrix())


def _peer_v_kernel(*refs, tm):
    off_refs = refs[:OFF_WAYS]
    w_ref, tab_ref, rep_ref, x1_ref, gf_ref, y_ref = refs[OFF_WAYS:OFF_WAYS + 6]
    bufs, w8_ref, po_ref = refs[OFF_WAYS + 6:-2], refs[-2], refs[-1]
    mask = _diag_mask()
    w8_ref[...] = _dot(w_ref[...], rep_ref[...], precision=HIGHEST)
    row8 = lax.broadcasted_iota(jnp.int32, (2 * SUB, LANES), 0)

    def gather(t, g_ref):
        return _gather_rows(off_refs, tab_ref, g_ref, t)

    def compute(t, g_ref):
        g16 = pltpu.bitcast(g_ref[...], BF16)
        wexp = jnp.where(mask, jnp.broadcast_to(w8_ref[pl.ds(t, 1), :], (2 * SUB, NCOL)), 0.0)
        o = _dot(_split_bf16(wexp), g16)
        o8 = o[0:2 * SUB] + o[2 * SUB:]
        pieces = [jnp.sum(jnp.where(row8 == j, o8, 0.0), axis=0, keepdims=True) for j in range(2 * SUB)]
        po_ref[pl.ds(t, 1), :] = jnp.concatenate(pieces, axis=1)

    _buffered_token_loop(tm, gather, compute, bufs)
    x = x1_ref[...] + po_ref[...]
    y_ref[...] = x * lax.rsqrt(jnp.mean(x * x, axis=-1, keepdims=True) + EPS) * gf_ref[...]


def _peer_v(eoff, w, tab, x1, g_final, tm=TM_PEER):
    n = eoff.shape[0]
    return pl.pallas_call(
        functools.partial(_peer_v_kernel, tm=tm),
        grid=(n // tm,),
        in_specs=_off_specs(tm) + [
                  pl.BlockSpec((tm, P_NSEL), lambda i: (i, 0)),
                  pl.BlockSpec(memory_space=pltpu.VMEM),
                  pl.BlockSpec((P_NSEL, NCOL), lambda i: (0, 0)),
                  pl.BlockSpec((tm, D), lambda i: (i, 0)),
                  pl.BlockSpec((1, D), lambda i: (0, 0))],
        out_specs=pl.BlockSpec((tm, D), lambda i: (i, 0)),
        out_shape=jax.ShapeDtypeStruct((n, D), F32),
        scratch_shapes=[pltpu.VMEM((NROWS, LANES), jnp.int32)] * PEER_BUFS
        + [pltpu.VMEM((tm, NCOL), F32), pltpu.VMEM((tm, D), F32)],
        compiler_params=_params(("arbitrary",)),
        name="peer_v",
    )(*_deal_offsets(eoff), w, tab, _group_matrix().T, x1, g_final.reshape(1, D))


def _pack_w_in(w_in):
    w = jnp.concatenate([w_in[:, 2048:3584], w_in[:, :2048], w_in[:, 3584:4096], w_in[:, 4112:6160],
                         w_in[:, 4096:4112], jnp.zeros((D, PW - C_SMALL - 16), w_in.dtype)], axis=1)
    return w.astype(BF16)


def _encoder(x, wts):
    b, s, _ = x.shape
    n = b * s
    x2 = x.reshape(n, D)
    p2 = _in_proj(x2, wts["g_mix"], wts["w_cat"])
    p3 = p2.reshape(b, s, PW)
    oa = _retention(p3, wts["ret_gn"]).reshape(n, HW)
    qkv, sm = _gdn_prep(p3, wts["dn_conv"], wts["dn_a_log"], wts["dn_dt_bias"])
    o_f, o_b = _gdn_scan(_gdn_terms(qkv, sm), nb=GDN_SCAN_SEQS if b % GDN_SCAN_SEQS == 0 else 1)
    x1 = _merge(x2, oa, o_f.reshape(n, HW), o_b.reshape(n, HW), p2, wts["dn_norm"],
                wts["wa"], wts["wb"], wts["wo"])
    h2, eoff, gate = _route(x1, wts["g_ffn"], wts["wq"], wts["k1"], wts["k2"])
    w = _peer_u(eoff, h2, gate, wts["tab_u"])
    return _peer_v(eoff, w, wts["tab_v"], x1, wts["g_final"]).reshape(b, s, D)


def kernel(x_prompt, x_sample, g_mix, w_in, ret_gn, dn_conv, dn_a_log, dn_dt_bias, dn_norm, w_branch_a, w_branch_b, w_out, g_ffn, peer_wq, peer_k1, peer_k2, peer_u, peer_v, g_final):
    wts = dict(
        g_mix=g_mix[0], w_cat=_pack_w_in(w_in[0]), ret_gn=ret_gn[0], dn_conv=dn_conv[0],
        dn_a_log=dn_a_log[0], dn_dt_bias=dn_dt_bias[0], dn_norm=dn_norm[0],
        wa=w_branch_a[0].astype(BF16), wb=w_branch_b[0].astype(BF16), wo=w_out[0].astype(BF16),
        g_ffn=g_ffn[0], wq=peer_wq[0].astype(BF16), k1=peer_k1[0].astype(BF16), k2=peer_k2[0].astype(BF16),
        tab_u=_pack_table(peer_u[0]), tab_v=_pack_table(peer_v[0]), g_final=g_final)
    return (_encoder(x_prompt, wts), _encoder(x_sample, wts))
```

```python
import functools

import jax
import jax.numpy as jnp
import numpy as np
from jax import lax
from jax.experimental import pallas as pl
from jax.experimental.pallas import tpu as pltpu

F32 = jnp.float32
BF16 = jnp.bfloat16
HIGHEST = lax.Precision.HIGHEST

D = 1024
EPS = 1e-6
NH = 4
HD = 128
HW = NH * HD
ROPE_BASE = 10000.0
DN_CONV = 5
DN_C = 64
RET_C = 512
P_HEADS = 8
P_DK = 256
P_NKEYS = 128
P_TOPK = 16
P_NSEL = P_HEADS * P_TOPK
LANES = 128
SUB = 4
V7X_VMEM_BYTES = 64 * 1024 * 1024
VMEM_LIMIT = V7X_VMEM_BYTES * 7 // 8

C_DNQKV = 0
C_RET = 1536
C_Z = 3584
C_MA = 4096
C_MB = 5120
C_SMALL = 6144
PW = 6272
PCH = 896

TM_IN_PROJ = 256
TM_GDN_PREP = 512
TM_MERGE = 512
TM_ROUTE = LANES
TM_PEER = 512
GDN_TERMS_CHUNKS = 4
GDN_SCAN_SEQS = 4


def _dot(a, b, **kw):
    return jnp.dot(a, b, preferred_element_type=F32, **kw)


def _dot_nt(a, b, **kw):
    return lax.dot_general(a, b, (((1,), (1,)), ((), ())), preferred_element_type=F32, **kw)


def _params(sem=None):
    return pltpu.CompilerParams(dimension_semantics=sem, vmem_limit_bytes=VMEM_LIMIT)


def _in_proj_kernel(x_ref, g_ref, w_ref, p_ref):
    x = x_ref[...]
    h = x * lax.rsqrt(jnp.mean(x * x, axis=-1, keepdims=True) + EPS) * g_ref[...]
    h = h.astype(BF16)
    for c in range(0, PW, PCH):
        p_ref[:, c:c + PCH] = _dot(h, w_ref[:, c:c + PCH])


def _in_proj(x2, g, w_cat, tm=TM_IN_PROJ):
    n = x2.shape[0]
    return pl.pallas_call(
        _in_proj_kernel,
        grid=(n // tm,),
        in_specs=[pl.BlockSpec((tm, D), lambda i: (i, 0)),
                  pl.BlockSpec((1, D), lambda i: (0, 0)),
                  pl.BlockSpec((D, PW), lambda i: (0, 0))],
        out_specs=pl.BlockSpec((tm, PW), lambda i: (i, 0)),
        out_shape=jax.ShapeDtypeStruct((n, PW), F32),
        compiler_params=_params(("parallel",)),
        name="in_proj",
    )(x2, g.reshape(1, D), w_cat)


def _ret_kernel(q_ref, k_ref, v_ref, gt_ref, cos_ref, sin_ref, dmat_ref, vec_ref, cdec_ref, gn_ref,
                o_ref, run_ref, stash_ref):
    ph = pl.program_id(1)
    c = pl.program_id(2)
    nc = pl.num_programs(2)

    @pl.when(c == 0)
    def _():
        run_ref[...] = jnp.zeros_like(run_ref)

    cos2 = cos_ref[...]
    sinm = sin_ref[...]

    def rot(t):
        return t * cos2 + pltpu.roll(t, HD // 2, axis=1) * sinm

    def state_update(h, kh, vh, zeta, dec_row):
        kz = (kh * zeta).T.astype(BF16)
        contrib = _dot(kz, vh.astype(BF16))
        run_ref[h] = run_ref[h] * cdec_ref[dec_row:dec_row + 1, :] + contrib

    @pl.when(ph == 0)
    def _():
        ci = nc - 1 - c
        for h in range(NH):
            sl = slice(h * HD, (h + 1) * HD)
            kh = rot(k_ref[0, :, sl])
            stash_ref[ci, h] = run_ref[h]
            state_update(h, kh, v_ref[0, :, sl], vec_ref[3, h], NH + h)

    @pl.when(ph == 1)
    def _():
        for h in range(NH):
            sl = slice(h * HD, (h + 1) * HD)
            qh = rot(q_ref[0, :, sl]) * (HD ** -0.5)
            kh = rot(k_ref[0, :, sl])
            vh = v_ref[0, :, sl]
            vb = vh.astype(BF16)
            scores = _dot_nt(qh.astype(BF16), kh.astype(BF16)) * dmat_ref[h]
            o = _dot(scores.astype(BF16), vb)
            o += _dot((qh * vec_ref[0, h]).astype(BF16), run_ref[h].astype(BF16))
            o += _dot((qh * vec_ref[1, h]).astype(BF16), stash_ref[c, h].astype(BF16))
            state_update(h, kh, vh, vec_ref[2, h], h)
            mu = jnp.mean(o, axis=-1, keepdims=True)
            oc = o - mu
            var = jnp.mean(oc * oc, axis=-1, keepdims=True)
            on = oc * lax.rsqrt(var + EPS) * gn_ref[:, sl]
            gate = gt_ref[0, :, sl]
            o_ref[0, :, sl] = (on * (gate * jax.nn.sigmoid(gate))).astype(o_ref.dtype)


def _ret_tables(s):
    half = HD // 2
    inv_freq = 1.0 / (ROPE_BASE ** jnp.linspace(0.0, 1.0, half, dtype=F32))
    ang = jnp.arange(s, dtype=F32)[:, None] * inv_freq[None, :]
    cos, sin = jnp.cos(ang), jnp.sin(ang)
    cos2 = jnp.concatenate([cos, cos], axis=-1)
    sinm = jnp.concatenate([-sin, sin], axis=-1)
    lg = jnp.log(1.0 - 2.0 ** (-5.0 - jnp.arange(NH, dtype=F32)))
    lgb = lg[::-1]
    c = RET_C
    pos = jnp.arange(c, dtype=F32)
    diff = pos[:, None] - pos[None, :]
    dmat = jnp.where(diff >= 0, jnp.exp(lg[:, None, None] * jnp.maximum(diff, 0.0)),
                     jnp.exp(lgb[:, None, None] * jnp.maximum(-diff, 0.0)))
    xi_f = jnp.exp(lg[:, None] * (pos + 1.0))
    xi_b = jnp.exp(lgb[:, None] * (c - pos))
    zeta_f = jnp.exp(lg[:, None] * (c - 1 - pos))
    zeta_b = jnp.exp(lgb[:, None] * pos)
    vec = jnp.broadcast_to(jnp.stack([xi_f, xi_b, zeta_f, zeta_b])[..., None], (4, NH, c, LANES))
    cdec = jnp.concatenate([jnp.exp(lg * c), jnp.exp(lgb * c)])
    cdec = jnp.broadcast_to(cdec[:, None], (2 * NH, LANES))
    return cos2, sinm, dmat, vec, cdec


def _retention(p3, ret_gn):
    b, s, _ = p3.shape
    c = RET_C
    nc = s // c
    cos2, sinm, dmat, vec, cdec = _ret_tables(s)

    def chunk(ph, ci):
        return jnp.where(ph == 0, nc - 1 - ci, ci)

    def pspec(col):
        return pl.BlockSpec((1, c, HW), lambda bi, ph, ci: (bi, chunk(ph, ci), col))

    base = C_RET // HW
    return pl.pallas_call(
        _ret_kernel,
        grid=(b, 2, nc),
        in_specs=[pspec(base), pspec(base + 1), pspec(base + 2), pspec(base + 3),
                  pl.BlockSpec((c, HD), lambda bi, ph, ci: (chunk(ph, ci), 0)),
                  pl.BlockSpec((c, HD), lambda bi, ph, ci: (chunk(ph, ci), 0)),
                  pl.BlockSpec((NH, c, c), lambda bi, ph, ci: (0, 0, 0)),
                  pl.BlockSpec((4, NH, c, LANES), lambda bi, ph, ci: (0, 0, 0, 0)),
                  pl.BlockSpec((2 * NH, LANES), lambda bi, ph, ci: (0, 0)),
                  pl.BlockSpec((1, HW), lambda bi, ph, ci: (0, 0))],
        out_specs=pl.BlockSpec((1, c, HW), lambda bi, ph, ci: (bi, ci * ph, 0)),
        out_shape=jax.ShapeDtypeStruct((b, s, HW), BF16),
        scratch_shapes=[pltpu.VMEM((NH, HD, HD), F32), pltpu.VMEM((nc, NH, HD, HD), F32)],
        compiler_params=_params(("parallel", "arbitrary", "arbitrary")),
        name="retention",
    )(p3, p3, p3, p3, cos2, sinm, dmat, vec, cdec, ret_gn.reshape(1, HW))


def _gdn_prep_kernel(prev_ref, cur_ref, next_ref, sm_ref, cw_ref, alog_ref, dtb_ref, qkv_ref, smo_ref, *, tc):
    i = pl.program_id(1)
    last = pl.num_programs(1) - 1
    prev = jnp.where(i == 0, 0.0, prev_ref[0])
    nxt = jnp.where(i == last, 0.0, next_ref[0])
    ext = jnp.concatenate([prev, cur_ref[0], nxt], axis=0)
    nrow = tc + 16
    half = DN_CONV // 2
    y = jnp.zeros((tc, 3 * HW), F32)
    for j in range(DN_CONV):
        shifted = ext if j == half else pltpu.roll(ext, (half - j) % nrow, axis=0)
        y = y + shifted[8:8 + tc] * cw_ref[j:j + 1, :]
    y = y * jax.nn.sigmoid(y)
    for h in range(2 * NH):
        sl = slice(h * HD, (h + 1) * HD)
        t = y[:, sl]
        t = t * lax.rsqrt(jnp.sum(t * t, axis=-1, keepdims=True) + EPS)
        if h < NH:
            t = t * (HD ** -0.5)
        qkv_ref[0, :, sl] = t
    qkv_ref[0, :, 2 * HW:] = y[:, 2 * HW:]
    a = sm_ref[0]
    lane = lax.broadcasted_iota(jnp.int32, a.shape, 1)
    z = a + dtb_ref[...]
    softplus = jnp.maximum(z, 0.0) + jnp.log1p(jnp.exp(-jnp.abs(z)))
    g = -jnp.exp(alog_ref[...]) * softplus
    smo_ref[0] = jnp.where(lane < 2 * NH, jax.nn.sigmoid(a), g)


def _gdn_prep(p3, conv_w, a_log, dt_bias, tc=TM_GDN_PREP):
    b, s, _ = p3.shape
    nt = s // tc
    r8 = tc // 8
    w3 = 3 * HW
    zeros8 = jnp.zeros((2 * NH,), F32)
    pad = jnp.zeros((LANES - 4 * NH,), F32)
    alog = jnp.concatenate([zeros8, a_log[0], a_log[1], pad]).reshape(1, LANES)
    dtb = jnp.concatenate([zeros8, dt_bias[0], dt_bias[1], pad]).reshape(1, LANES)
    return pl.pallas_call(
        functools.partial(_gdn_prep_kernel, tc=tc),
        grid=(b, nt),
        in_specs=[pl.BlockSpec((1, 8, w3), lambda bi, i: (bi, jnp.maximum(i * r8 - 1, 0), 0)),
                  pl.BlockSpec((1, tc, w3), lambda bi, i: (bi, i, 0)),
                  pl.BlockSpec((1, 8, w3), lambda bi, i: (bi, jnp.minimum((i + 1) * r8, s // 8 - 1), 0)),
                  pl.BlockSpec((1, tc, LANES), lambda bi, i: (bi, i, C_SMALL // LANES)),
                  pl.BlockSpec((DN_CONV, w3), lambda bi, i: (0, 0)),
                  pl.BlockSpec((1, LANES), lambda bi, i: (0, 0)),
                  pl.BlockSpec((1, LANES), lambda bi, i: (0, 0))],
        out_specs=[pl.BlockSpec((1, tc, w3), lambda bi, i: (bi, i, 0)),
                   pl.BlockSpec((1, tc, LANES), lambda bi, i: (bi, i, 0))],
        out_shape=[jax.ShapeDtypeStruct((b, s, w3), F32), jax.ShapeDtypeStruct((b, s, LANES), F32)],
        compiler_params=_params(("parallel", "parallel")),
        name="gdn_prep",
    )(p3, p3, p3, p3, conv_w, alog, dtb)


def _gdn_terms_kernel(qkv_ref, sm_ref, *out_refs, nchunk):
    c = DN_C
    row = lax.broadcasted_iota(jnp.int32, (c, c), 0)
    col = lax.broadcasted_iota(jnp.int32, (c, c), 1)
    lane = lax.broadcasted_iota(jnp.int32, (c, LANES), 1)
    gt_ref = out_refs[-1]
    ids, qn, kn, vv, gcum, beta, lmat, strict = [], [], [], [], [], [], [], []
    for ci in range(nchunk):
        rows = slice(ci * c, (ci + 1) * c)
        sm = sm_ref[0, rows, :]
        for d in range(2):
            tri = jnp.where((row <= col) if d else (row >= col), 1.0, 0.0)
            gc_all = _dot(tri, sm, precision=HIGHEST)
            gc_t = gc_all.T
            incl = (row <= col) if d else (row >= col)
            for h in range(NH):
                lg = 2 * NH + NH * d + h
                ids.append((ci, d, h))
                qn.append(qkv_ref[0, rows, h * HD:(h + 1) * HD])
                kn.append(qkv_ref[0, rows, HW + h * HD:HW + (h + 1) * HD])
                vv.append(qkv_ref[0, rows, 2 * HW + h * HD:2 * HW + (h + 1) * HD])
                beta.append(jnp.sum(jnp.where(lane == NH * d + h, sm, 0.0), axis=-1, keepdims=True))
                g = jnp.sum(jnp.where(lane == lg, gc_all, 0.0), axis=-1, keepdims=True)
                gcum.append(g)
                lmat.append(jnp.where(incl, jnp.exp(jnp.where(incl, g - gc_t[lg:lg + 1, :], 0.0)), 0.0))
                strict.append((row < col) if d else (row > col))
    n = len(ids)
    rng = range(n)
    kb = [kn[i] * beta[i] for i in rng]
    knb = [kn[i].astype(BF16) for i in rng]
    kk = [_dot_nt(kb[i].astype(BF16), knb[i]) for i in rng]
    a = [jnp.where(strict[i], kk[i] * lmat[i], 0.0) for i in rng]
    ab = [a[i].astype(BF16) for i in rng]
    p = [_dot(ab[i], ab[i]) for i in rng]
    m = [-a[i] for i in rng]
    steps = int(np.log2(c)) - 1
    for it in range(steps):
        pb = [p[i].astype(BF16) for i in rng]
        mp = [_dot(m[i].astype(BF16), pb[i]) for i in rng]
        m = [m[i] + p[i] + mp[i] for i in rng]
        if it + 1 < steps:
            p = [_dot(pb[i], pb[i]) for i in rng]
    eg = [jnp.exp(gcum[i]) for i in rng]
    rhs = [jnp.concatenate([vv[i] * beta[i], kb[i] * eg[i]], axis=-1) for i in rng]
    corr = [_dot(m[i].astype(BF16), rhs[i].astype(BF16)) for i in rng]
    attn = [_dot_nt(qn[i].astype(BF16), knb[i]) for i in rng]
    for i, (ci, d, h) in enumerate(ids):
        u_ref, w_ref, qd_ref, kdt_ref, att_ref = out_refs[5 * d:5 * d + 5]
        rows = slice(ci * c, (ci + 1) * c)
        sl = slice(h * HD, (h + 1) * HD)
        sol = rhs[i] + corr[i]
        gl = gcum[i][0:1, :] if d else gcum[i][c - 1:c, :]
        u_ref[0, rows, sl] = sol[:, :HD]
        w_ref[0, rows, sl] = sol[:, HD:].astype(BF16)
        qd_ref[0, rows, sl] = (qn[i] * eg[i]).astype(BF16)
        kdt_ref[0, ci, h] = (kn[i] * jnp.exp(gl - gcum[i])).T.astype(BF16)
        att_ref[0, ci, h] = (attn[i] * lmat[i]).astype(BF16)
        gt_ref[0, ci, NH * d + h:NH * d + h + 1, :] = jnp.broadcast_to(jnp.exp(gl), (1, LANES))


def _gdn_terms(qkv, sm, nchunk=GDN_TERMS_CHUNKS):
    b, s, w3 = qkv.shape
    c = DN_C
    nc = s // c
    r = nchunk * c
    tok = lambda dt: (pl.BlockSpec((1, r, HW), lambda bi, i: (bi, i, 0)), jax.ShapeDtypeStruct((b, s, HW), dt))
    kdt = (pl.BlockSpec((1, nchunk, NH, HD, c), lambda bi, i: (bi, i, 0, 0, 0)),
           jax.ShapeDtypeStruct((b, nc, NH, HD, c), BF16))
    att = (pl.BlockSpec((1, nchunk, NH, c, c), lambda bi, i: (bi, i, 0, 0, 0)),
           jax.ShapeDtypeStruct((b, nc, NH, c, c), BF16))
    gt = (pl.BlockSpec((1, nchunk, 2 * NH, LANES), lambda bi, i: (bi, i, 0, 0)),
          jax.ShapeDtypeStruct((b, nc, 2 * NH, LANES), F32))
    outs = [tok(F32), tok(BF16), tok(BF16), kdt, att] * 2 + [gt]
    return pl.pallas_call(
        functools.partial(_gdn_terms_kernel, nchunk=nchunk),
        grid=(b, nc // nchunk),
        in_specs=[pl.BlockSpec((1, r, w3), lambda bi, i: (bi, i, 0)),
                  pl.BlockSpec((1, r, LANES), lambda bi, i: (bi, i, 0))],
        out_specs=[o[0] for o in outs],
        out_shape=[o[1] for o in outs],
        compiler_params=_params(("parallel", "parallel")),
        name="gdn_terms",
    )(qkv, sm)


def _gdn_scan_kernel(*refs, nb):
    (uf, wf, qdf, kdtf, attf, gtf, ub, wb, qdb, kdtb, attb, gtb, of_ref, ob_ref, s_ref) = refs

    @pl.when(pl.program_id(1) == 0)
    def _():
        s_ref[...] = jnp.zeros_like(s_ref)

    dirs = ((uf, wf, qdf, kdtf, attf, gtf, of_ref), (ub, wb, qdb, kdtb, attb, gtb, ob_ref))
    ids = [(bi, d, h) for bi in range(nb) for d in range(2) for h in range(NH)]
    sl = [slice(h * HD, (h + 1) * HD) for h in range(NH)]
    s = [s_ref[bi, d, h] for bi, d, h in ids]
    sb = [t.astype(BF16) for t in s]
    ws = [_dot(dirs[d][1][bi, :, sl[h]], sb[i]) for i, (bi, d, h) in enumerate(ids)]
    qs = [_dot(dirs[d][2][bi, :, sl[h]], sb[i]) for i, (bi, d, h) in enumerate(ids)]
    vnb = [(dirs[d][0][bi, :, sl[h]] - ws[i]).astype(BF16) for i, (bi, d, h) in enumerate(ids)]
    av = [_dot(dirs[d][4][bi, 0, h], vnb[i]) for i, (bi, d, h) in enumerate(ids)]
    kv = [_dot(dirs[d][3][bi, 0, h], vnb[i]) for i, (bi, d, h) in enumerate(ids)]
    for i, (bi, d, h) in enumerate(ids):
        dirs[d][6][bi, :, sl[h]] = qs[i] + av[i]
        gt = dirs[d][5][bi, 0, NH * d + h:NH * d + h + 1, :]
        s_ref[bi, d, h] = s[i] * gt + kv[i]


def _gdn_scan(terms, nb):
    uf, wf, qdf, kdtf, attf, ub, wb, qdb, kdtb, attb, gt = terms
    b, s, _ = uf.shape
    c = DN_C
    nc = s // c
    fwd = lambda bi, i: i
    bwd = lambda bi, i: nc - 1 - i

    def specs(pos):
        return [pl.BlockSpec((nb, c, HW), lambda bi, i: (bi, pos(bi, i), 0)),
                pl.BlockSpec((nb, c, HW), lambda bi, i: (bi, pos(bi, i), 0)),
                pl.BlockSpec((nb, c, HW), lambda bi, i: (bi, pos(bi, i), 0)),
                pl.BlockSpec((nb, 1, NH, HD, c), lambda bi, i: (bi, pos(bi, i), 0, 0, 0)),
                pl.BlockSpec((nb, 1, NH, c, c), lambda bi, i: (bi, pos(bi, i), 0, 0, 0)),
                pl.BlockSpec((nb, 1, 2 * NH, LANES), lambda bi, i: (bi, pos(bi, i), 0, 0))]

    return pl.pallas_call(
        functools.partial(_gdn_scan_kernel, nb=nb),
        grid=(b // nb, nc),
        in_specs=specs(fwd) + specs(bwd),
        out_specs=[pl.BlockSpec((nb, c, HW), lambda bi, i: (bi, i, 0)),
                   pl.BlockSpec((nb, c, HW), lambda bi, i: (bi, nc - 1 - i, 0))],
        out_shape=[jax.ShapeDtypeStruct((b, s, HW), F32), jax.ShapeDtypeStruct((b, s, HW), F32)],
        scratch_shapes=[pltpu.VMEM((nb, 2, NH, HD, HD), F32)],
        compiler_params=_params(("parallel", "arbitrary")),
        name="gdn_scan",
    )(uf, wf, qdf, kdtf, attf, gt, ub, wb, qdb, kdtb, attb, gt)


def _merge_kernel(x_ref, oa_ref, of_ref, ob_ref, z_ref, ma_ref, mb_ref, ng_ref, wa_ref, wb_ref, wo_ref, x1_ref):
    ob_parts = []
    for h in range(NH):
        sl = slice(h * HD, (h + 1) * HD)
        o = of_ref[:, sl] + ob_ref[:, sl]
        o = o * lax.rsqrt(jnp.mean(o * o, axis=-1, keepdims=True) + EPS) * ng_ref[...]
        z = z_ref[:, sl]
        ob_parts.append((o * (z * jax.nn.sigmoid(z))).astype(BF16))
    ob = jnp.concatenate(ob_parts, axis=-1)
    ya = _dot(oa_ref[...], wa_ref[...])
    yb = _dot(ob, wb_ref[...])
    merged = jax.nn.sigmoid(ma_ref[...]) * ya + jax.nn.sigmoid(mb_ref[...]) * yb
    x1_ref[...] = x_ref[...] + _dot(merged.astype(BF16), wo_ref[...])


def _merge(x2, oa, o_f, o_b, p2, norm_g, wa, wb, wo, tm=TM_MERGE):
    n = x2.shape[0]
    const = lambda i: (0, 0)
    return pl.pallas_call(
        _merge_kernel,
        grid=(n // tm,),
        in_specs=[pl.BlockSpec((tm, D), lambda i: (i, 0)),
                  pl.BlockSpec((tm, HW), lambda i: (i, 0)),
                  pl.BlockSpec((tm, HW), lambda i: (i, 0)),
                  pl.BlockSpec((tm, HW), lambda i: (i, 0)),
                  pl.BlockSpec((tm, HW), lambda i: (i, C_Z // HW)),
                  pl.BlockSpec((tm, D), lambda i: (i, C_MA // D)),
                  pl.BlockSpec((tm, D), lambda i: (i, C_MB // D)),
                  pl.BlockSpec((1, HD), const),
                  pl.BlockSpec((HW, D), const),
                  pl.BlockSpec((HW, D), const),
                  pl.BlockSpec((D, D), const)],
        out_specs=pl.BlockSpec((tm, D), lambda i: (i, 0)),
        out_shape=jax.ShapeDtypeStruct((n, D), F32),
        compiler_params=_params(("parallel",)),
        name="merge",
    )(x2, oa, o_f, o_b, p2, p2, p2, norm_g.reshape(1, HD), wa, wb, wo)


def _top_rows(arrays, payload=None):
    r = arrays[0].shape[0]
    row = lax.broadcasted_iota(jnp.int32, arrays[0].shape, 0).astype(F32)
    arrays = list(arrays)
    out = [([], []) for _ in arrays]
    for _ in range(P_TOPK):
        for a, s in enumerate(arrays):
            m = jnp.max(s, axis=0, keepdims=True)
            first = jnp.min(jnp.where(s == m, row, float(r)), axis=0, keepdims=True)
            sel = row == first
            out[a][0].append(m)
            out[a][1].append(first if payload is None
                             else jnp.max(jnp.where(sel, payload, -1.0), axis=0, keepdims=True))
            arrays[a] = jnp.where(sel, -jnp.inf, s)
    return out


def _stack_rows(rows):
    t = rows[0].shape[1]
    row = lax.broadcasted_iota(jnp.int32, (len(rows), t), 0)
    out = jnp.zeros((len(rows), t), F32)
    for i, r in enumerate(rows):
        out = jnp.where(row == i, r, out)
    return out


def _pair_candidates(v1, i1, v2, i2):
    v1s, i1s, v2s, i2s = _stack_rows(v1), _stack_rows(i1), _stack_rows(v2), _stack_rows(i2)
    brow = lax.broadcasted_iota(jnp.int32, (8, v2s.shape[1]), 0)
    cand = [v1[0] + v2s]
    cidx = [i1[0] * P_NKEYS + i2s]
    for a in range(1, 8):
        ok = brow < P_TOPK // (a + 1)
        cand.append(jnp.where(ok, v1[a] + v2s[0:8], -jnp.inf))
        cidx.append(i1[a] * P_NKEYS + i2s[0:8])
    cand.append(v1s[8:] + v2[0])
    cidx.append(i1s[8:] * P_NKEYS + i2[0])
    return jnp.concatenate(cand, axis=0), jnp.concatenate(cidx, axis=0)


def _route_kernel(x1_ref, g_ref, wq_ref, k1_ref, k2_ref, h2_ref, eidx_ref, gate_ref):
    x = x1_ref[...]
    h2 = x * lax.rsqrt(jnp.mean(x * x, axis=-1, keepdims=True) + EPS) * g_ref[...]
    h2_ref[...] = h2
    qp = _dot(h2.astype(BF16), wq_ref[...])
    gates, eidxs = [], []
    for h in range(P_HEADS):
        q1 = qp[:, h * P_DK:h * P_DK + P_DK // 2].astype(BF16)
        q2 = qp[:, h * P_DK + P_DK // 2:(h + 1) * P_DK].astype(BF16)
        s1 = _dot_nt(k1_ref[h], q1)
        s2 = _dot_nt(k2_ref[h], q2)
        (v1, i1), (v2, i2) = _top_rows([s1, s2])
        cand, cidx = _pair_candidates(v1, i1, v2, i2)
        (sc, ei), = _top_rows([cand], cidx)
        ex = jnp.exp(_stack_rows(sc) - sc[0])
        gates.append(ex / jnp.sum(ex, axis=0, keepdims=True))
        eidxs.append(_stack_rows(ei) * float(SUB))
    gate_ref[...] = jnp.concatenate(gates, axis=0).T
    c = lax.broadcasted_iota(jnp.int32, (P_NSEL, P_NSEL), 0)
    s = lax.broadcasted_iota(jnp.int32, (P_NSEL, P_NSEL), 1)
    perm = jnp.where(s == OFF_WAYS * (c % OFF_PER) + c // OFF_PER, 1.0, 0.0)
    dealt = _dot(perm, jnp.concatenate(eidxs, axis=0), precision=HIGHEST)
    eidx_ref[...] = dealt.T.astype(jnp.int32)


def _route(x1, g_ffn, wq, k1, k2, tm=TM_ROUTE):
    n = x1.shape[0]
    return pl.pallas_call(
        _route_kernel,
        grid=(n // tm,),
        in_specs=[pl.BlockSpec((tm, D), lambda i: (i, 0)),
                  pl.BlockSpec((1, D), lambda i: (0, 0)),
                  pl.BlockSpec((D, P_HEADS * P_DK), lambda i: (0, 0)),
                  pl.BlockSpec((P_HEADS, P_NKEYS, P_DK // 2), lambda i: (0, 0, 0)),
                  pl.BlockSpec((P_HEADS, P_NKEYS, P_DK // 2), lambda i: (0, 0, 0))],
        out_specs=[pl.BlockSpec((tm, D), lambda i: (i, 0)),
                   pl.BlockSpec((tm, P_NSEL), lambda i: (i, 0)),
                   pl.BlockSpec((tm, P_NSEL), lambda i: (i, 0))],
        out_shape=[jax.ShapeDtypeStruct((n, D), F32),
                   jax.ShapeDtypeStruct((n, P_NSEL), jnp.int32),
                   jax.ShapeDtypeStruct((n, P_NSEL), F32)],
        compiler_params=_params(("parallel",)),
        name="route",
    )(x1, g_ffn.reshape(1, D), wq, k1, k2)


def _pack_table(t):
    bits = lax.bitcast_convert_type(t.astype(BF16), jnp.uint16).astype(jnp.uint32)
    packed = bits[:, :D // 2] | (bits[:, D // 2:] << 16)
    return lax.bitcast_convert_type(packed, jnp.int32).reshape(t.shape[0] * SUB, LANES)


PEER_UNROLL = 16
NROWS = P_NSEL * SUB
NCOL = 2 * NROWS


OFF_WAYS = 8
OFF_PER = P_NSEL // OFF_WAYS


def _deal_offsets(eoff):
    n = eoff.shape[0]
    return [eoff[:, OFF_PER * j:OFF_PER * (j + 1)].reshape(n * OFF_PER) for j in range(OFF_WAYS)]


def _gather_rows(off_refs, tab_ref, g_ref, t):
    off = 0
    for i in range(OFF_PER):
        pos = t * OFF_PER + i
        for j, off_ref in enumerate(off_refs):
            k = OFF_WAYS * i + j
            off = pl.multiple_of(off_ref[pos], SUB)
            g_ref[SUB * k:SUB * (k + 1), :] = tab_ref[pl.ds(off, SUB), :]
    return off


PEER_BUFS = 2


def _buffered_token_loop(tm, gather, compute, bufs):
    lag = len(bufs) - 1
    last = 0
    for j in range(lag):
        last = gather(j + jnp.minimum(last, 0), bufs[j])

    def body(i, carry):
        t = PEER_UNROLL * i
        last = 0
        for j in range(PEER_UNROLL):
            compute(t + j, bufs[j % len(bufs)])
            nxt = jnp.minimum(t + j + lag, tm - 1) + jnp.minimum(last, 0)
            last = gather(nxt, bufs[(j + lag) % len(bufs)])
        return carry

    lax.fori_loop(0, tm // PEER_UNROLL, body, 0)


def _diag_mask():
    j = lax.broadcasted_iota(jnp.int32, (2 * SUB, NCOL), 0)
    c = lax.broadcasted_iota(jnp.int32, (2 * SUB, NCOL), 1)
    return (c & 7) == ((j & 3) * 2 + (j >> 2))


def _split_bf16(a):
    hi = a.astype(BF16)
    lo = (a - hi.astype(F32)).astype(BF16)
    return jnp.concatenate([hi, lo], axis=0)


def _peer_u_kernel(*refs, tm):
    off_refs = refs[:OFF_WAYS]
    x_ref, gate_ref, tab_ref, sel_ref, w_ref = refs[OFF_WAYS:OFF_WAYS + 5]
    bufs, r_ref = refs[OFF_WAYS + 5:-1], refs[-1]
    mask = _diag_mask()

    def gather(t, g_ref):
        return _gather_rows(off_refs, tab_ref, g_ref, t)

    def compute(t, g_ref):
        g16 = pltpu.bitcast(g_ref[...], BF16)
        x8 = x_ref[pl.ds(t, 1), :].reshape(2 * SUB, LANES)
        o = _dot_nt(_split_bf16(x8), g16)
        o8 = o[0:2 * SUB] + o[2 * SUB:]
        r_ref[pl.ds(t, 1), :] = jnp.sum(jnp.where(mask, o8, 0.0), axis=0, keepdims=True)

    _buffered_token_loop(tm, gather, compute, bufs)
    pre = _dot(r_ref[...], sel_ref[...], precision=HIGHEST)
    act = 0.5 * pre * (1.0 + lax.erf(pre * (2.0 ** -0.5)))
    w_ref[...] = gate_ref[...] * act


def _group_matrix():
    c = jnp.arange(NCOL, dtype=jnp.int32)[:, None] // 8
    k = jnp.arange(P_NSEL, dtype=jnp.int32)[None, :]
    return (c == k).astype(F32)


def _off_specs(tm):
    return [pl.BlockSpec((tm * OFF_PER,), lambda i: (i,), memory_space=pltpu.SMEM) for _ in range(OFF_WAYS)]


def _peer_u(eoff, h2, gate, tab, tm=TM_PEER):
    n = eoff.shape[0]
    return pl.pallas_call(
        functools.partial(_peer_u_kernel, tm=tm),
        grid=(n // tm,),
        in_specs=_off_specs(tm) + [
                  pl.BlockSpec((tm, D), lambda i: (i, 0)),
                  pl.BlockSpec((tm, P_NSEL), lambda i: (i, 0)),
                  pl.BlockSpec(memory_space=pltpu.VMEM),
                  pl.BlockSpec((NCOL, P_NSEL), lambda i: (0, 0))],
        out_specs=pl.BlockSpec((tm, P_NSEL), lambda i: (i, 0)),
        out_shape=jax.ShapeDtypeStruct((n, P_NSEL), F32),
        scratch_shapes=[pltpu.VMEM((NROWS, LANES), jnp.int32)] * PEER_BUFS + [pltpu.VMEM((tm, NCOL), F32)],
        compiler_params=_params(("arbitrary",)),
        name="peer_u",
    )(*_deal_offsets(eoff), h2, gate, tab, _group_matrix())


def _peer_v_kernel(*refs, tm):
    off_refs = refs[:OFF_WAYS]
    w_ref, tab_ref, rep_ref, x1_ref, gf_ref, y_ref = refs[OFF_WAYS:OFF_WAYS + 6]
    bufs, w8_ref, po_ref = refs[OFF_WAYS + 6:-2], refs[-2], refs[-1]
    mask = _diag_mask()
    w8_ref[...] = _dot(w_ref[...], rep_ref[...], precision=HIGHEST)
    row8 = lax.broadcasted_iota(jnp.int32, (2 * SUB, LANES), 0)

    def gather(t, g_ref):
        return _gather_rows(off_refs, tab_ref, g_ref, t)

    def compute(t, g_ref):
        g16 = pltpu.bitcast(g_ref[...], BF16)
        wexp = jnp.where(mask, jnp.broadcast_to(w8_ref[pl.ds(t, 1), :], (2 * SUB, NCOL)), 0.0)
        o = _dot(_split_bf16(wexp), g16)
        o8 = o[0:2 * SUB] + o[2 * SUB:]
        pieces = [jnp.sum(jnp.where(row8 == j, o8, 0.0), axis=0, keepdims=True) for j in range(2 * SUB)]
        po_ref[pl.ds(t, 1), :] = jnp.concatenate(pieces, axis=1)

    _buffered_token_loop(tm, gather, compute, bufs)
    x = x1_ref[...] + po_ref[...]
    y_ref[...] = x * lax.rsqrt(jnp.mean(x * x, axis=-1, keepdims=True) + EPS) * gf_ref[...]


def _peer_v(eoff, w, tab, x1, g_final, tm=TM_PEER):
    n = eoff.shape[0]
    return pl.pallas_call(
        functools.partial(_peer_v_kernel, tm=tm),
        grid=(n // tm,),
        in_specs=_off_specs(tm) + [
                  pl.BlockSpec((tm, P_NSEL), lambda i: (i, 0)),
                  pl.BlockSpec(memory_space=pltpu.VMEM),
                  pl.BlockSpec((P_NSEL, NCOL), lambda i: (0, 0)),
                  pl.BlockSpec((tm, D), lambda i: (i, 0)),
                  pl.BlockSpec((1, D), lambda i: (0, 0))],
        out_specs=pl.BlockSpec((tm, D), lambda i: (i, 0)),
        out_shape=jax.ShapeDtypeStruct((n, D), F32),
        scratch_shapes=[pltpu.VMEM((NROWS, LANES), jnp.int32)] * PEER_BUFS
        + [pltpu.VMEM((tm, NCOL), F32), pltpu.VMEM((tm, D), F32)],
        compiler_params=_params(("arbitrary",)),
        name="peer_v",
    )(*_deal_offsets(eoff), w, tab, _group_matrix().T, x1, g_final.reshape(1, D))


def _pack_w_in(w_in):
    w = jnp.concatenate([w_in[:, 2048:3584], w_in[:, :2048], w_in[:, 3584:4096], w_in[:, 4112:6160],
                         w_in[:, 4096:4112], jnp.zeros((D, PW - C_SMALL - 16), w_in.dtype)], axis=1)
    return w.astype(BF16)


def _encoder(x, wts):
    b, s, _ = x.shape
    n = b * s
    x2 = x.reshape(n, D)
    p2 = _in_proj(x2, wts["g_mix"], wts["w_cat"])
    p3 = p2.reshape(b, s, PW)
    oa = _retention(p3, wts["ret_gn"]).reshape(n, HW)
    qkv, sm = _gdn_prep(p3, wts["dn_conv"], wts["dn_a_log"], wts["dn_dt_bias"])
    o_f, o_b = _gdn_scan(_gdn_terms(qkv, sm), nb=GDN_SCAN_SEQS if b % GDN_SCAN_SEQS == 0 else 1)
    x1 = _merge(x2, oa, o_f.reshape(n, HW), o_b.reshape(n, HW), p2, wts["dn_norm"],
                wts["wa"], wts["wb"], wts["wo"])
    h2, eoff, gate = _route(x1, wts["g_ffn"], wts["wq"], wts["k1"], wts["k2"])
    w = _peer_u(eoff, h2, gate, wts["tab_u"])
    return _peer_v(eoff, w, wts["tab_v"], x1, wts["g_final"]).reshape(b, s, D)


def kernel(x_prompt, x_sample, g_mix, w_in, ret_gn, dn_conv, dn_a_log, dn_dt_bias, dn_norm, w_branch_a, w_branch_b, w_out, g_ffn, peer_wq, peer_k1, peer_k2, peer_u, peer_v, g_final):
    wts = dict(
        g_mix=g_mix[0], w_cat=_pack_w_in(w_in[0]), ret_gn=ret_gn[0], dn_conv=dn_conv[0],
        dn_a_log=dn_a_log[0], dn_dt_bias=dn_dt_bias[0], dn_norm=dn_norm[0],
        wa=w_branch_a[0].astype(BF16), wb=w_branch_b[0].astype(BF16), wo=w_out[0].astype(BF16),
        g_ffn=g_ffn[0], wq=peer_wq[0].astype(BF16), k1=peer_k1[0].astype(BF16), k2=peer_k2[0].astype(BF16),
        tab_u=_pack_table(peer_u[0]), tab_v=_pack_table(peer_v[0]), g_final=g_final)
    return (_encoder(x_prompt, wts), _encoder(x_sample, wts))
```

```python
import functools

import jax
import jax.numpy as jnp
import numpy as np
from jax import lax
from jax.experimental import pallas as pl
from jax.experimental.pallas import tpu as pltpu

F32 = jnp.float32
BF16 = jnp.bfloat16
HIGHEST = lax.Precision.HIGHEST

D = 1024
EPS = 1e-6
NH = 4
HD = 128
HW = NH * HD
ROPE_BASE = 10000.0
DN_CONV = 5
DN_C = 64
RET_C = 512
P_HEADS = 8
P_DK = 256
P_NKEYS = 128
P_TOPK = 16
P_NSEL = P_HEADS * P_TOPK
LANES = 128
SUB = 4
V7X_VMEM_BYTES = 64 * 1024 * 1024
VMEM_LIMIT = V7X_VMEM_BYTES * 7 // 8

C_DNQKV = 0
C_RET = 1536
C_Z = 3584
C_MA = 4096
C_MB = 5120
C_SMALL = 6144
PW = 6272
PCH = 896

TM_IN_PROJ = 256
TM_GDN_PREP = 512
TM_MERGE = 512
TM_ROUTE = LANES
TM_PEER = 512
GDN_TERMS_CHUNKS = 4
GDN_SCAN_SEQS = 4


def _dot(a, b, **kw):
    return jnp.dot(a, b, preferred_element_type=F32, **kw)


def _dot_nt(a, b, **kw):
    return lax.dot_general(a, b, (((1,), (1,)), ((), ())), preferred_element_type=F32, **kw)


def _params(sem=None):
    return pltpu.CompilerParams(dimension_semantics=sem, vmem_limit_bytes=VMEM_LIMIT)


def _in_proj_kernel(x_ref, g_ref, w_ref, p_ref):
    x = x_ref[...]
    h = x * lax.rsqrt(jnp.mean(x * x, axis=-1, keepdims=True) + EPS) * g_ref[...]
    h = h.astype(BF16)
    for c in range(0, PW, PCH):
        p_ref[:, c:c + PCH] = _dot(h, w_ref[:, c:c + PCH])


def _in_proj(x2, g, w_cat, tm=TM_IN_PROJ):
    n = x2.shape[0]
    return pl.pallas_call(
        _in_proj_kernel,
        grid=(n // tm,),
        in_specs=[pl.BlockSpec((tm, D), lambda i: (i, 0)),
                  pl.BlockSpec((1, D), lambda i: (0, 0)),
                  pl.BlockSpec((D, PW), lambda i: (0, 0))],
        out_specs=pl.BlockSpec((tm, PW), lambda i: (i, 0)),
        out_shape=jax.ShapeDtypeStruct((n, PW), F32),
        compiler_params=_params(("parallel",)),
        name="in_proj",
    )(x2, g.reshape(1, D), w_cat)


def _ret_kernel(q_ref, k_ref, v_ref, gt_ref, cos_ref, sin_ref, dmat_ref, vec_ref, cdec_ref, gn_ref,
                o_ref, run_ref, stash_ref):
    ph = pl.program_id(1)
    c = pl.program_id(2)
    nc = pl.num_programs(2)

    @pl.when(c == 0)
    def _():
        run_ref[...] = jnp.zeros_like(run_ref)

    cos2 = cos_ref[...]
    sinm = sin_ref[...]

    def rot(t):
        return t * cos2 + pltpu.roll(t, HD // 2, axis=1) * sinm

    def state_update(h, kh, vh, zeta, dec_row):
        kz = (kh * zeta).T.astype(BF16)
        contrib = _dot(kz, vh.astype(BF16))
        run_ref[h] = run_ref[h] * cdec_ref[dec_row:dec_row + 1, :] + contrib

    @pl.when(ph == 0)
    def _():
        ci = nc - 1 - c
        for h in range(NH):
            sl = slice(h * HD, (h + 1) * HD)
            kh = rot(k_ref[0, :, sl])
            stash_ref[ci, h] = run_ref[h]
            state_update(h, kh, v_ref[0, :, sl], vec_ref[3, h], NH + h)

    @pl.when(ph == 1)
    def _():
        for h in range(NH):
            sl = slice(h * HD, (h + 1) * HD)
            qh = rot(q_ref[0, :, sl]) * (HD ** -0.5)
            kh = rot(k_ref[0, :, sl])
            vh = v_ref[0, :, sl]
            vb = vh.astype(BF16)
            scores = _dot_nt(qh.astype(BF16), kh.astype(BF16)) * dmat_ref[h]
            o = _dot(scores.astype(BF16), vb)
            o += _dot((qh * vec_ref[0, h]).astype(BF16), run_ref[h].astype(BF16))
            o += _dot((qh * vec_ref[1, h]).astype(BF16), stash_ref[c, h].astype(BF16))
            state_update(h, kh, vh, vec_ref[2, h], h)
            mu = jnp.mean(o, axis=-1, keepdims=True)
            oc = o - mu
            var = jnp.mean(oc * oc, axis=-1, keepdims=True)
            on = oc * lax.rsqrt(var + EPS) * gn_ref[:, sl]
            gate = gt_ref[0, :, sl]
            o_ref[0, :, sl] = (on * (gate * jax.nn.sigmoid(gate))).astype(o_ref.dtype)


def _ret_tables(s):
    half = HD // 2
    inv_freq = 1.0 / (ROPE_BASE ** jnp.linspace(0.0, 1.0, half, dtype=F32))
    ang = jnp.arange(s, dtype=F32)[:, None] * inv_freq[None, :]
    cos, sin = jnp.cos(ang), jnp.sin(ang)
    cos2 = jnp.concatenate([cos, cos], axis=-1)
    sinm = jnp.concatenate([-sin, sin], axis=-1)
    lg = jnp.log(1.0 - 2.0 ** (-5.0 - jnp.arange(NH, dtype=F32)))
    lgb = lg[::-1]
    c = RET_C
    pos = jnp.arange(c, dtype=F32)
    diff = pos[:, None] - pos[None, :]
    dmat = jnp.where(diff >= 0, jnp.exp(lg[:, None, None] * jnp.maximum(diff, 0.0)),
                     jnp.exp(lgb[:, None, None] * jnp.maximum(-diff, 0.0)))
    xi_f = jnp.exp(lg[:, None] * (pos + 1.0))
    xi_b = jnp.exp(lgb[:, None] * (c - pos))
    zeta_f = jnp.exp(lg[:, None] * (c - 1 - pos))
    zeta_b = jnp.exp(lgb[:, None] * pos)
    vec = jnp.broadcast_to(jnp.stack([xi_f, xi_b, zeta_f, zeta_b])[..., None], (4, NH, c, LANES))
    cdec = jnp.concatenate([jnp.exp(lg * c), jnp.exp(lgb * c)])
    cdec = jnp.broadcast_to(cdec[:, None], (2 * NH, LANES))
    return cos2, sinm, dmat, vec, cdec


def _retention(p3, ret_gn):
    b, s, _ = p3.shape
    c = RET_C
    nc = s // c
    cos2, sinm, dmat, vec, cdec = _ret_tables(s)

    def chunk(ph, ci):
        return jnp.where(ph == 0, nc - 1 - ci, ci)

    def pspec(col):
        return pl.BlockSpec((1, c, HW), lambda bi, ph, ci: (bi, chunk(ph, ci), col))

    base = C_RET // HW
    return pl.pallas_call(
        _ret_kernel,
        grid=(b, 2, nc),
        in_specs=[pspec(base), pspec(base + 1), pspec(base + 2), pspec(base + 3),
                  pl.BlockSpec((c, HD), lambda bi, ph, ci: (chunk(ph, ci), 0)),
                  pl.BlockSpec((c, HD), lambda bi, ph, ci: (chunk(ph, ci), 0)),
                  pl.BlockSpec((NH, c, c), lambda bi, ph, ci: (0, 0, 0)),
                  pl.BlockSpec((4, NH, c, LANES), lambda bi, ph, ci: (0, 0, 0, 0)),
                  pl.BlockSpec((2 * NH, LANES), lambda bi, ph, ci: (0, 0)),
                  pl.BlockSpec((1, HW), lambda bi, ph, ci: (0, 0))],
        out_specs=pl.BlockSpec((1, c, HW), lambda bi, ph, ci: (bi, ci * ph, 0)),
        out_shape=jax.ShapeDtypeStruct((b, s, HW), BF16),
        scratch_shapes=[pltpu.VMEM((NH, HD, HD), F32), pltpu.VMEM((nc, NH, HD, HD), F32)],
        compiler_params=_params(("parallel", "arbitrary", "arbitrary")),
        name="retention",
    )(p3, p3, p3, p3, cos2, sinm, dmat, vec, cdec, ret_gn.reshape(1, HW))


def _gdn_prep_kernel(prev_ref, cur_ref, next_ref, sm_ref, cw_ref, alog_ref, dtb_ref, qkv_ref, smo_ref, *, tc):
    i = pl.program_id(1)
    last = pl.num_programs(1) - 1
    prev = jnp.where(i == 0, 0.0, prev_ref[0])
    nxt = jnp.where(i == last, 0.0, next_ref[0])
    ext = jnp.concatenate([prev, cur_ref[0], nxt], axis=0)
    nrow = tc + 16
    half = DN_CONV // 2
    y = jnp.zeros((tc, 3 * HW), F32)
    for j in range(DN_CONV):
        shifted = ext if j == half else pltpu.roll(ext, (half - j) % nrow, axis=0)
        y = y + shifted[8:8 + tc] * cw_ref[j:j + 1, :]
    y = y * jax.nn.sigmoid(y)
    for h in range(2 * NH):
        sl = slice(h * HD, (h + 1) * HD)
        t = y[:, sl]
        t = t * lax.rsqrt(jnp.sum(t * t, axis=-1, keepdims=True) + EPS)
        if h < NH:
            t = t * (HD ** -0.5)
        qkv_ref[0, :, sl] = t
    qkv_ref[0, :, 2 * HW:] = y[:, 2 * HW:]
    a = sm_ref[0]
    lane = lax.broadcasted_iota(jnp.int32, a.shape, 1)
    z = a + dtb_ref[...]
    softplus = jnp.maximum(z, 0.0) + jnp.log1p(jnp.exp(-jnp.abs(z)))
    g = -jnp.exp(alog_ref[...]) * softplus
    smo_ref[0] = jnp.where(lane < 2 * NH, jax.nn.sigmoid(a), g)


def _gdn_prep(p3, conv_w, a_log, dt_bias, tc=TM_GDN_PREP):
    b, s, _ = p3.shape
    nt = s // tc
    r8 = tc // 8
    w3 = 3 * HW
    zeros8 = jnp.zeros((2 * NH,), F32)
    pad = jnp.zeros((LANES - 4 * NH,), F32)
    alog = jnp.concatenate([zeros8, a_log[0], a_log[1], pad]).reshape(1, LANES)
    dtb = jnp.concatenate([zeros8, dt_bias[0], dt_bias[1], pad]).reshape(1, LANES)
    return pl.pallas_call(
        functools.partial(_gdn_prep_kernel, tc=tc),
        grid=(b, nt),
        in_specs=[pl.BlockSpec((1, 8, w3), lambda bi, i: (bi, jnp.maximum(i * r8 - 1, 0), 0)),
                  pl.BlockSpec((1, tc, w3), lambda bi, i: (bi, i, 0)),
                  pl.BlockSpec((1, 8, w3), lambda bi, i: (bi, jnp.minimum((i + 1) * r8, s // 8 - 1), 0)),
                  pl.BlockSpec((1, tc, LANES), lambda bi, i: (bi, i, C_SMALL // LANES)),
                  pl.BlockSpec((DN_CONV, w3), lambda bi, i: (0, 0)),
                  pl.BlockSpec((1, LANES), lambda bi, i: (0, 0)),
                  pl.BlockSpec((1, LANES), lambda bi, i: (0, 0))],
        out_specs=[pl.BlockSpec((1, tc, w3), lambda bi, i: (bi, i, 0)),
                   pl.BlockSpec((1, tc, LANES), lambda bi, i: (bi, i, 0))],
        out_shape=[jax.ShapeDtypeStruct((b, s, w3), F32), jax.ShapeDtypeStruct((b, s, LANES), F32)],
        compiler_params=_params(("parallel", "parallel")),
        name="gdn_prep",
    )(p3, p3, p3, p3, conv_w, alog, dtb)


def _gdn_terms_kernel(qkv_ref, sm_ref, *out_refs, nchunk):
    c = DN_C
    row = lax.broadcasted_iota(jnp.int32, (c, c), 0)
    col = lax.broadcasted_iota(jnp.int32, (c, c), 1)
    lane = lax.broadcasted_iota(jnp.int32, (c, LANES), 1)
    gt_ref = out_refs[-1]
    ids, qn, kn, vv, gcum, beta, lmat, strict = [], [], [], [], [], [], [], []
    for ci in range(nchunk):
        rows = slice(ci * c, (ci + 1) * c)
        sm = sm_ref[0, rows, :]
        for d in range(2):
            tri = jnp.where((row <= col) if d else (row >= col), 1.0, 0.0)
            gc_all = _dot(tri, sm, precision=HIGHEST)
            gc_t = gc_all.T
            incl = (row <= col) if d else (row >= col)
            for h in range(NH):
                lg = 2 * NH + NH * d + h
                ids.append((ci, d, h))
                qn.append(qkv_ref[0, rows, h * HD:(h + 1) * HD])
                kn.append(qkv_ref[0, rows, HW + h * HD:HW + (h + 1) * HD])
                vv.append(qkv_ref[0, rows, 2 * HW + h * HD:2 * HW + (h + 1) * HD])
                beta.append(jnp.sum(jnp.where(lane == NH * d + h, sm, 0.0), axis=-1, keepdims=True))
                g = jnp.sum(jnp.where(lane == lg, gc_all, 0.0), axis=-1, keepdims=True)
                gcum.append(g)
                lmat.append(jnp.where(incl, jnp.exp(jnp.where(incl, g - gc_t[lg:lg + 1, :], 0.0)), 0.0))
                strict.append((row < col) if d else (row > col))
    n = len(ids)
    rng = range(n)
    kb = [kn[i] * beta[i] for i in rng]
    knb = [kn[i].astype(BF16) for i in rng]
    kk = [_dot_nt(kb[i].astype(BF16), knb[i]) for i in rng]
    a = [jnp.where(strict[i], kk[i] * lmat[i], 0.0) for i in rng]
    ab = [a[i].astype(BF16) for i in rng]
    p = [_dot(ab[i], ab[i]) for i in rng]
    m = [-a[i] for i in rng]
    steps = int(np.log2(c)) - 1
    for it in range(steps):
        pb = [p[i].astype(BF16) for i in rng]
        mp = [_dot(m[i].astype(BF16), pb[i]) for i in rng]
        m = [m[i] + p[i] + mp[i] for i in rng]
        if it + 1 < steps:
            p = [_dot(pb[i], pb[i]) for i in rng]
    eg = [jnp.exp(gcum[i]) for i in rng]
    rhs = [jnp.concatenate([vv[i] * beta[i], kb[i] * eg[i]], axis=-1) for i in rng]
    corr = [_dot(m[i].astype(BF16), rhs[i].astype(BF16)) for i in rng]
    attn = [_dot_nt(qn[i].astype(BF16), knb[i]) for i in rng]
    for i, (ci, d, h) in enumerate(ids):
        u_ref, w_ref, qd_ref, kdt_ref, att_ref = out_refs[5 * d:5 * d + 5]
        rows = slice(ci * c, (ci + 1) * c)
        sl = slice(h * HD, (h + 1) * HD)
        sol = rhs[i] + corr[i]
        gl = gcum[i][0:1, :] if d else gcum[i][c - 1:c, :]
        u_ref[0, rows, sl] = sol[:, :HD]
        w_ref[0, rows, sl] = sol[:, HD:].astype(BF16)
        qd_ref[0, rows, sl] = (qn[i] * eg[i]).astype(BF16)
        kdt_ref[0, ci, h] = (kn[i] * jnp.exp(gl - gcum[i])).T.astype(BF16)
        att_ref[0, ci, h] = (attn[i] * lmat[i]).astype(BF16)
        gt_ref[0, ci, NH * d + h:NH * d + h + 1, :] = jnp.broadcast_to(jnp.exp(gl), (1, LANES))


def _gdn_terms(qkv, sm, nchunk=GDN_TERMS_CHUNKS):
    b, s, w3 = qkv.shape
    c = DN_C
    nc = s // c
    r = nchunk * c
    tok = lambda dt: (pl.BlockSpec((1, r, HW), lambda bi, i: (bi, i, 0)), jax.ShapeDtypeStruct((b, s, HW), dt))
    kdt = (pl.BlockSpec((1, nchunk, NH, HD, c), lambda bi, i: (bi, i, 0, 0, 0)),
           jax.ShapeDtypeStruct((b, nc, NH, HD, c), BF16))
    att = (pl.BlockSpec((1, nchunk, NH, c, c), lambda bi, i: (bi, i, 0, 0, 0)),
           jax.ShapeDtypeStruct((b, nc, NH, c, c), BF16))
    gt = (pl.BlockSpec((1, nchunk, 2 * NH, LANES), lambda bi, i: (bi, i, 0, 0)),
          jax.ShapeDtypeStruct((b, nc, 2 * NH, LANES), F32))
    outs = [tok(F32), tok(BF16), tok(BF16), kdt, att] * 2 + [gt]
    return pl.pallas_call(
        functools.partial(_gdn_terms_kernel, nchunk=nchunk),
        grid=(b, nc // nchunk),
        in_specs=[pl.BlockSpec((1, r, w3), lambda bi, i: (bi, i, 0)),
                  pl.BlockSpec((1, r, LANES), lambda bi, i: (bi, i, 0))],
        out_specs=[o[0] for o in outs],
        out_shape=[o[1] for o in outs],
        compiler_params=_params(("parallel", "parallel")),
        name="gdn_terms",
    )(qkv, sm)


def _gdn_scan_kernel(*refs, nb):
    (uf, wf, qdf, kdtf, attf, gtf, ub, wb, qdb, kdtb, attb, gtb, of_ref, ob_ref, s_ref) = refs

    @pl.when(pl.program_id(1) == 0)
    def _():
        s_ref[...] = jnp.zeros_like(s_ref)

    dirs = ((uf, wf, qdf, kdtf, attf, gtf, of_ref), (ub, wb, qdb, kdtb, attb, gtb, ob_ref))
    ids = [(bi, d, h) for bi in range(nb) for d in range(2) for h in range(NH)]
    sl = [slice(h * HD, (h + 1) * HD) for h in range(NH)]
    s = [s_ref[bi, d, h] for bi, d, h in ids]
    sb = [t.astype(BF16) for t in s]
    ws = [_dot(dirs[d][1][bi, :, sl[h]], sb[i]) for i, (bi, d, h) in enumerate(ids)]
    qs = [_dot(dirs[d][2][bi, :, sl[h]], sb[i]) for i, (bi, d, h) in enumerate(ids)]
    vnb = [(dirs[d][0][bi, :, sl[h]] - ws[i]).astype(BF16) for i, (bi, d, h) in enumerate(ids)]
    av = [_dot(dirs[d][4][bi, 0, h], vnb[i]) for i, (bi, d, h) in enumerate(ids)]
    kv = [_dot(dirs[d][3][bi, 0, h], vnb[i]) for i, (bi, d, h) in enumerate(ids)]
    for i, (bi, d, h) in enumerate(ids):
        dirs[d][6][bi, :, sl[h]] = qs[i] + av[i]
        gt = dirs[d][5][bi, 0, NH * d + h:NH * d + h + 1, :]
        s_ref[bi, d, h] = s[i] * gt + kv[i]


def _gdn_scan(terms, nb):
    uf, wf, qdf, kdtf, attf, ub, wb, qdb, kdtb, attb, gt = terms
    b, s, _ = uf.shape
    c = DN_C
    nc = s // c
    fwd = lambda bi, i: i
    bwd = lambda bi, i: nc - 1 - i

    def specs(pos):
        return [pl.BlockSpec((nb, c, HW), lambda bi, i: (bi, pos(bi, i), 0)),
                pl.BlockSpec((nb, c, HW), lambda bi, i: (bi, pos(bi, i), 0)),
                pl.BlockSpec((nb, c, HW), lambda bi, i: (bi, pos(bi, i), 0)),
                pl.BlockSpec((nb, 1, NH, HD, c), lambda bi, i: (bi, pos(bi, i), 0, 0, 0)),
                pl.BlockSpec((nb, 1, NH, c, c), lambda bi, i: (bi, pos(bi, i), 0, 0, 0)),
                pl.BlockSpec((nb, 1, 2 * NH, LANES), lambda bi, i: (bi, pos(bi, i), 0, 0))]

    return pl.pallas_call(
        functools.partial(_gdn_scan_kernel, nb=nb),
        grid=(b // nb, nc),
        in_specs=specs(fwd) + specs(bwd),
        out_specs=[pl.BlockSpec((nb, c, HW), lambda bi, i: (bi, i, 0)),
                   pl.BlockSpec((nb, c, HW), lambda bi, i: (bi, nc - 1 - i, 0))],
        out_shape=[jax.ShapeDtypeStruct((b, s, HW), F32), jax.ShapeDtypeStruct((b, s, HW), F32)],
        scratch_shapes=[pltpu.VMEM((nb, 2, NH, HD, HD), F32)],
        compiler_params=_params(("parallel", "arbitrary")),
        name="gdn_scan",
    )(uf, wf, qdf, kdtf, attf, gt, ub, wb, qdb, kdtb, attb, gt)


def _merge_kernel(x_ref, oa_ref, of_ref, ob_ref, z_ref, ma_ref, mb_ref, ng_ref, wa_ref, wb_ref, wo_ref, x1_ref):
    ob_parts = []
    for h in range(NH):
        sl = slice(h * HD, (h + 1) * HD)
        o = of_ref[:, sl] + ob_ref[:, sl]
        o = o * lax.rsqrt(jnp.mean(o * o, axis=-1, keepdims=True) + EPS) * ng_ref[...]
        z = z_ref[:, sl]
        ob_parts.append((o * (z * jax.nn.sigmoid(z))).astype(BF16))
    ob = jnp.concatenate(ob_parts, axis=-1)
    ya = _dot(oa_ref[...], wa_ref[...])
    yb = _dot(ob, wb_ref[...])
    merged = jax.nn.sigmoid(ma_ref[...]) * ya + jax.nn.sigmoid(mb_ref[...]) * yb
    x1_ref[...] = x_ref[...] + _dot(merged.astype(BF16), wo_ref[...])


def _merge(x2, oa, o_f, o_b, p2, norm_g, wa, wb, wo, tm=TM_MERGE):
    n = x2.shape[0]
    const = lambda i: (0, 0)
    return pl.pallas_call(
        _merge_kernel,
        grid=(n // tm,),
        in_specs=[pl.BlockSpec((tm, D), lambda i: (i, 0)),
                  pl.BlockSpec((tm, HW), lambda i: (i, 0)),
                  pl.BlockSpec((tm, HW), lambda i: (i, 0)),
                  pl.BlockSpec((tm, HW), lambda i: (i, 0)),
                  pl.BlockSpec((tm, HW), lambda i: (i, C_Z // HW)),
                  pl.BlockSpec((tm, D), lambda i: (i, C_MA // D)),
                  pl.BlockSpec((tm, D), lambda i: (i, C_MB // D)),
                  pl.BlockSpec((1, HD), const),
                  pl.BlockSpec((HW, D), const),
                  pl.BlockSpec((HW, D), const),
                  pl.BlockSpec((D, D), const)],
        out_specs=pl.BlockSpec((tm, D), lambda i: (i, 0)),
        out_shape=jax.ShapeDtypeStruct((n, D), F32),
        compiler_params=_params(("parallel",)),
        name="merge",
    )(x2, oa, o_f, o_b, p2, p2, p2, norm_g.reshape(1, HD), wa, wb, wo)


def _top_rows(arrays, payload=None):
    r = arrays[0].shape[0]
    row = lax.broadcasted_iota(jnp.int32, arrays[0].shape, 0).astype(F32)
    arrays = list(arrays)
    out = [([], []) for _ in arrays]
    for _ in range(P_TOPK):
        for a, s in enumerate(arrays):
            m = jnp.max(s, axis=0, keepdims=True)
            first = jnp.min(jnp.where(s == m, row, float(r)), axis=0, keepdims=True)
            sel = row == first
            out[a][0].append(m)
            out[a][1].append(first if payload is None
                             else jnp.max(jnp.where(sel, payload, -1.0), axis=0, keepdims=True))
            arrays[a] = jnp.where(sel, -jnp.inf, s)
    return out


def _stack_rows(rows):
    t = rows[0].shape[1]
    row = lax.broadcasted_iota(jnp.int32, (len(rows), t), 0)
    out = jnp.zeros((len(rows), t), F32)
    for i, r in enumerate(rows):
        out = jnp.where(row == i, r, out)
    return out


def _pair_candidates(v1, i1, v2, i2):
    v1s, i1s, v2s, i2s = _stack_rows(v1), _stack_rows(i1), _stack_rows(v2), _stack_rows(i2)
    brow = lax.broadcasted_iota(jnp.int32, (8, v2s.shape[1]), 0)
    cand = [v1[0] + v2s]
    cidx = [i1[0] * P_NKEYS + i2s]
    for a in range(1, 8):
        ok = brow < P_TOPK // (a + 1)
        cand.append(jnp.where(ok, v1[a] + v2s[0:8], -jnp.inf))
        cidx.append(i1[a] * P_NKEYS + i2s[0:8])
    cand.append(v1s[8:] + v2[0])
    cidx.append(i1s[8:] * P_NKEYS + i2[0])
    return jnp.concatenate(cand, axis=0), jnp.concatenate(cidx, axis=0)


def _route_kernel(x1_ref, g_ref, wq_ref, k1_ref, k2_ref, h2_ref, eidx_ref, gate_ref):
    x = x1_ref[...]
    h2 = x * lax.rsqrt(jnp.mean(x * x, axis=-1, keepdims=True) + EPS) * g_ref[...]
    h2_ref[...] = h2
    qp = _dot(h2.astype(BF16), wq_ref[...])
    gates, eidxs = [], []
    for h in range(P_HEADS):
        q1 = qp[:, h * P_DK:h * P_DK + P_DK // 2].astype(BF16)
        q2 = qp[:, h * P_DK + P_DK // 2:(h + 1) * P_DK].astype(BF16)
        s1 = _dot_nt(k1_ref[h], q1)
        s2 = _dot_nt(k2_ref[h], q2)
        (v1, i1), (v2, i2) = _top_rows([s1, s2])
        cand, cidx = _pair_candidates(v1, i1, v2, i2)
        (sc, ei), = _top_rows([cand], cidx)
        ex = jnp.exp(_stack_rows(sc) - sc[0])
        gates.append(ex / jnp.sum(ex, axis=0, keepdims=True))
        eidxs.append(_stack_rows(ei) * float(SUB))
    gate_ref[...] = jnp.concatenate(gates, axis=0).T
    c = lax.broadcasted_iota(jnp.int32, (P_NSEL, P_NSEL), 0)
    s = lax.broadcasted_iota(jnp.int32, (P_NSEL, P_NSEL), 1)
    perm = jnp.where(s == OFF_WAYS * (c % OFF_PER) + c // OFF_PER, 1.0, 0.0)
    dealt = _dot(perm, jnp.concatenate(eidxs, axis=0), precision=HIGHEST)
    eidx_ref[...] = dealt.T.astype(jnp.int32)


def _route(x1, g_ffn, wq, k1, k2, tm=TM_ROUTE):
    n = x1.shape[0]
    return pl.pallas_call(
        _route_kernel,
        grid=(n // tm,),
        in_specs=[pl.BlockSpec((tm, D), lambda i: (i, 0)),
                  pl.BlockSpec((1, D), lambda i: (0, 0)),
                  pl.BlockSpec((D, P_HEADS * P_DK), lambda i: (0, 0)),
                  pl.BlockSpec((P_HEADS, P_NKEYS, P_DK // 2), lambda i: (0, 0, 0)),
                  pl.BlockSpec((P_HEADS, P_NKEYS, P_DK // 2), lambda i: (0, 0, 0))],
        out_specs=[pl.BlockSpec((tm, D), lambda i: (i, 0)),
                   pl.BlockSpec((tm, P_NSEL), lambda i: (i, 0)),
                   pl.BlockSpec((tm, P_NSEL), lambda i: (i, 0))],
        out_shape=[jax.ShapeDtypeStruct((n, D), F32),
                   jax.ShapeDtypeStruct((n, P_NSEL), jnp.int32),
                   jax.ShapeDtypeStruct((n, P_NSEL), F32)],
        compiler_params=_params(("parallel",)),
        name="route",
    )(x1, g_ffn.reshape(1, D), wq, k1, k2)


def _pack_table(t):
    bits = lax.bitcast_convert_type(t.astype(BF16), jnp.uint16).astype(jnp.uint32)
    packed = bits[:, :D // 2] | (bits[:, D // 2:] << 16)
    return lax.bitcast_convert_type(packed, jnp.int32).reshape(t.shape[0] * SUB, LANES)


PEER_UNROLL = 16
NROWS = P_NSEL * SUB
NCOL = 2 * NROWS


OFF_WAYS = 8
OFF_PER = P_NSEL // OFF_WAYS


def _deal_offsets(eoff):
    n = eoff.shape[0]
    return [eoff[:, OFF_PER * j:OFF_PER * (j + 1)].reshape(n * OFF_PER) for j in range(OFF_WAYS)]


def _gather_rows(off_refs, tab_ref, g_ref, t):
    off = 0
    for i in range(OFF_PER):
        pos = t * OFF_PER + i
        for j, off_ref in enumerate(off_refs):
            k = OFF_WAYS * i + j
            off = pl.multiple_of(off_ref[pos], SUB)
            g_ref[SUB * k:SUB * (k + 1), :] = tab_ref[pl.ds(off, SUB), :]
    return off


PEER_BUFS = 2


def _buffered_token_loop(tm, gather, compute, bufs):
    lag = len(bufs) - 1
    assert PEER_UNROLL % len(bufs) == 0 and tm % PEER_UNROLL == 0
    last = 0
    for j in range(lag):
        last = gather(j + jnp.minimum(last, 0), bufs[j])

    def body(i, carry):
        t = PEER_UNROLL * i
        last = 0
        for j in range(PEER_UNROLL):
            compute(t + j, bufs[j % len(bufs)])
            nxt = jnp.minimum(t + j + lag, tm - 1) + jnp.minimum(last, 0)
            last = gather(nxt, bufs[(j + lag) % len(bufs)])
        return carry

    lax.fori_loop(0, tm // PEER_UNROLL, body, 0)


def _diag_mask():
    j = lax.broadcasted_iota(jnp.int32, (2 * SUB, NCOL), 0)
    c = lax.broadcasted_iota(jnp.int32, (2 * SUB, NCOL), 1)
    return (c & 7) == ((j & 3) * 2 + (j >> 2))


def _split_bf16(a):
    hi = a.astype(BF16)
    lo = (a - hi.astype(F32)).astype(BF16)
    return jnp.concatenate([hi, lo], axis=0)


def _peer_u_kernel(*refs, tm):
    off_refs = refs[:OFF_WAYS]
    x_ref, gate_ref, tab_ref, sel_ref, w_ref = refs[OFF_WAYS:OFF_WAYS + 5]
    bufs, r_ref = refs[OFF_WAYS + 5:-1], refs[-1]
    mask = _diag_mask()

    def gather(t, g_ref):
        return _gather_rows(off_refs, tab_ref, g_ref, t)

    def compute(t, g_ref):
        g16 = pltpu.bitcast(g_ref[...], BF16)
        x8 = x_ref[pl.ds(t, 1), :].reshape(2 * SUB, LANES)
        o = _dot_nt(_split_bf16(x8), g16)
        o8 = o[0:2 * SUB] + o[2 * SUB:]
        r_ref[pl.ds(t, 1), :] = jnp.sum(jnp.where(mask, o8, 0.0), axis=0, keepdims=True)

    _buffered_token_loop(tm, gather, compute, bufs)
    pre = _dot(r_ref[...], sel_ref[...], precision=HIGHEST)
    act = 0.5 * pre * (1.0 + lax.erf(pre * (2.0 ** -0.5)))
    w_ref[...] = gate_ref[...] * act


def _group_matrix():
    c = jnp.arange(NCOL, dtype=jnp.int32)[:, None] // 8
    k = jnp.arange(P_NSEL, dtype=jnp.int32)[None, :]
    return (c == k).astype(F32)


def _off_specs(tm):
    return [pl.BlockSpec((tm * OFF_PER,), lambda i: (i,), memory_space=pltpu.SMEM) for _ in range(OFF_WAYS)]


def _peer_u(eoff, h2, gate, tab, tm=TM_PEER):
    n = eoff.shape[0]
    return pl.pallas_call(
        functools.partial(_peer_u_kernel, tm=tm),
        grid=(n // tm,),
        in_specs=_off_specs(tm) + [
                  pl.BlockSpec((tm, D), lambda i: (i, 0)),
                  pl.BlockSpec((tm, P_NSEL), lambda i: (i, 0)),
                  pl.BlockSpec(memory_space=pltpu.VMEM),
                  pl.BlockSpec((NCOL, P_NSEL), lambda i: (0, 0))],
        out_specs=pl.BlockSpec((tm, P_NSEL), lambda i: (i, 0)),
        out_shape=jax.ShapeDtypeStruct((n, P_NSEL), F32),
        scratch_shapes=[pltpu.VMEM((NROWS, LANES), jnp.int32)] * PEER_BUFS + [pltpu.VMEM((tm, NCOL), F32)],
        compiler_params=_params(("arbitrary",)),
        name="peer_u",
    )(*_deal_offsets(eoff), h2, gate, tab, _group_matrix())


def _peer_v_kernel(*refs, tm):
    off_refs = refs[:OFF_WAYS]
    w_ref, tab_ref, rep_ref, x1_ref, gf_ref, y_ref = refs[OFF_WAYS:OFF_WAYS + 6]
    bufs, w8_ref, po_ref = refs[OFF_WAYS + 6:-2], refs[-2], refs[-1]
    mask = _diag_mask()
    w8_ref[...] = _dot(w_ref[...], rep_ref[...], precision=HIGHEST)
    row8 = lax.broadcasted_iota(jnp.int32, (2 * SUB, LANES), 0)

    def gather(t, g_ref):
        return _gather_rows(off_refs, tab_ref, g_ref, t)

    def compute(t, g_ref):
        g16 = pltpu.bitcast(g_ref[...], BF16)
        wexp = jnp.where(mask, jnp.broadcast_to(w8_ref[pl.ds(t, 1), :], (2 * SUB, NCOL)), 0.0)
        o = _dot(_split_bf16(wexp), g16)
        o8 = o[0:2 * SUB] + o[2 * SUB:]
        pieces = [jnp.sum(jnp.where(row8 == j, o8, 0.0), axis=0, keepdims=True) for j in range(2 * SUB)]
        po_ref[pl.ds(t, 1), :] = jnp.concatenate(pieces, axis=1)

    _buffered_token_loop(tm, gather, compute, bufs)
    x = x1_ref[...] + po_ref[...]
    y_ref[...] = x * lax.rsqrt(jnp.mean(x * x, axis=-1, keepdims=True) + EPS) * gf_ref[...]


def _peer_v(eoff, w, tab, x1, g_final, tm=TM_PEER):
    n = eoff.shape[0]
    return pl.pallas_call(
        functools.partial(_peer_v_kernel, tm=tm),
        grid=(n // tm,),
        in_specs=_off_specs(tm) + [
                  pl.BlockSpec((tm, P_NSEL), lambda i: (i, 0)),
                  pl.BlockSpec(memory_space=pltpu.VMEM),
                  pl.BlockSpec((P_NSEL, NCOL), lambda i: (0, 0)),
                  pl.BlockSpec((tm, D), lambda i: (i, 0)),
                  pl.BlockSpec((1, D), lambda i: (0, 0))],
        out_specs=pl.BlockSpec((tm, D), lambda i: (i, 0)),
        out_shape=jax.ShapeDtypeStruct((n, D), F32),
        scratch_shapes=[pltpu.VMEM((NROWS, LANES), jnp.int32)] * PEER_BUFS
        + [pltpu.VMEM((tm, NCOL), F32), pltpu.VMEM((tm, D), F32)],
        compiler_params=_params(("arbitrary",)),
        name="peer_v",
    )(*_deal_offsets(eoff), w, tab, _group_matrix().T, x1, g_final.reshape(1, D))


def _pack_w_in(w_in):
    w = jnp.concatenate([w_in[:, 2048:3584], w_in[:, :2048], w_in[:, 3584:4096], w_in[:, 4112:6160],
                         w_in[:, 4096:4112], jnp.zeros((D, PW - C_SMALL - 16), w_in.dtype)], axis=1)
    return w.astype(BF16)


def _encoder(x, wts):
    b, s, _ = x.shape
    n = b * s
    x2 = x.reshape(n, D)
    p2 = _in_proj(x2, wts["g_mix"], wts["w_cat"])
    p3 = p2.reshape(b, s, PW)
    oa = _retention(p3, wts["ret_gn"]).reshape(n, HW)
    qkv, sm = _gdn_prep(p3, wts["dn_conv"], wts["dn_a_log"], wts["dn_dt_bias"])
    o_f, o_b = _gdn_scan(_gdn_terms(qkv, sm), nb=GDN_SCAN_SEQS if b % GDN_SCAN_SEQS == 0 else 1)
    x1 = _merge(x2, oa, o_f.reshape(n, HW), o_b.reshape(n, HW), p2, wts["dn_norm"],
                wts["wa"], wts["wb"], wts["wo"])
    h2, eoff, gate = _route(x1, wts["g_ffn"], wts["wq"], wts["k1"], wts["k2"])
    w = _peer_u(eoff, h2, gate, wts["tab_u"])
    return _peer_v(eoff, w, wts["tab_v"], x1, wts["g_final"]).reshape(b, s, D)


def kernel(x_prompt, x_sample, g_mix, w_in, ret_gn, dn_conv, dn_a_log, dn_dt_bias, dn_norm, w_branch_a, w_branch_b, w_out, g_ffn, peer_wq, peer_k1, peer_k2, peer_u, peer_v, g_final):
    wts = dict(
        g_mix=g_mix[0], w_cat=_pack_w_in(w_in[0]), ret_gn=ret_gn[0], dn_conv=dn_conv[0],
        dn_a_log=dn_a_log[0], dn_dt_bias=dn_dt_bias[0], dn_norm=dn_norm[0],
        wa=w_branch_a[0].astype(BF16), wb=w_branch_b[0].astype(BF16), wo=w_out[0].astype(BF16),
        g_ffn=g_ffn[0], wq=peer_wq[0].astype(BF16), k1=peer_k1[0].astype(BF16), k2=peer_k2[0].astype(BF16),
        tab_u=_pack_table(peer_u[0]), tab_v=_pack_table(peer_v[0]), g_final=g_final)
    return (_encoder(x_prompt, wts), _encoder(x_sample, wts))
```
